```python
import math
import jax, jax.numpy as jnp
from jax import lax
import numpy as np

D_MODEL = 1024
BATCH = 32
SEQ = 256
DEPTH = 2
DEC_BATCH = 4
DEC_SEQ = 2048
PAST_LEN = 512

GRID_W = 64
HEAD_DIM = 64
MIX_W = D_MODEL
SGU_W = MIX_W // 4
SGU_GROUPS = SGU_W // HEAD_DIM
SGU_CHUNK = 128
DIFF_HEADS = (MIX_W // 4) // HEAD_DIM
DIFF_DV = HEAD_DIM
DIFF_DQK = HEAD_DIM // 2
DIFF_SCALE = DIFF_DQK ** -0.5
GQA_HEADS = (MIX_W // 2) // HEAD_DIM
GQA_KV_HEADS = GQA_HEADS // 4
GQA_GROUP = GQA_HEADS // GQA_KV_HEADS
GQA_SCALE = HEAD_DIM ** -0.5
IN_SIZES = (SGU_W, SGU_W, DIFF_HEADS * 2 * DIFF_DQK, DIFF_HEADS * 2 * DIFF_DQK, DIFF_HEADS * DIFF_DV, GQA_HEADS * HEAD_DIM, GQA_KV_HEADS * HEAD_DIM, GQA_KV_HEADS * HEAD_DIM)
IN_W = 2 * SGU_W + 2 * DIFF_HEADS * 2 * DIFF_DQK + DIFF_HEADS * DIFF_DV + (GQA_HEADS + 2 * GQA_KV_HEADS) * HEAD_DIM
QBLOCK = 128
ADA_CHUNKS = 6
N_KEYS = 128
N_EXPERTS = N_KEYS * N_KEYS
PEER_HEADS = 8
PEER_TOPK = 16
PEER_QDIM = 128
PEER_HALF = PEER_QDIM // 2
TOKEN_BLOCK = 128
ROPE_THETA = 10000.0
EPS = 1e-6

kernel_name = 'hybrid_diffusion_prefix_step'


def rms_norm(x, g):
    xf = x.astype(jnp.float32)
    y = xf * lax.rsqrt(jnp.mean(xf * xf, axis=-1, keepdims=True) + EPS)
    return (y * g.astype(jnp.float32)).astype(x.dtype)


def axial_rope(n_tokens, dim):
    rows = n_tokens // GRID_W
    row = jnp.repeat(jnp.arange(rows, dtype=jnp.float32), GRID_W)
    col = jnp.tile(jnp.arange(GRID_W, dtype=jnp.float32), rows)
    quarter = dim // 4
    inv = ROPE_THETA ** (-jnp.arange(quarter, dtype=jnp.float32) / quarter)
    ang = jnp.concatenate([row[:, None] * inv, col[:, None] * inv], axis=-1)
    return jnp.cos(ang), jnp.sin(ang)


def apply_rope(x, cos, sin):
    shape = (cos.shape[0],) + (1,) * (x.ndim - 3) + (cos.shape[1],)
    c = cos.reshape(shape).astype(x.dtype)
    s = sin.reshape(shape).astype(x.dtype)
    x1, x2 = jnp.split(x, 2, axis=-1)
    return jnp.concatenate([x1 * c - x2 * s, x2 * c + x1 * s], axis=-1)


def sweep_query_blocks(fn, q):
    B, L = q.shape[0], q.shape[1]
    nb = L // QBLOCK
    qb = jnp.moveaxis(q.reshape((B, nb, QBLOCK) + q.shape[2:]), 1, 0)
    out = jnp.moveaxis(lax.map(fn, qb), 0, 1)
    return out.reshape((B, L) + out.shape[3:])


def peer_ffn(h, wq, subkeys, u_tab, v_tab):
    B, L, D = h.shape
    xt = h.reshape(B * L // TOKEN_BLOCK, TOKEN_BLOCK, D)

    def block(xb):
        T = xb.shape[0]
        q = (xb @ wq).reshape(T, PEER_HEADS, 2, PEER_HALF)
        s = jnp.einsum('thxk,xnk->thxn', q, subkeys).astype(jnp.float32)
        sv, si = lax.top_k(s, PEER_TOPK)
        cand = (sv[:, :, 0, :, None] + sv[:, :, 1, None, :]).reshape(T, PEER_HEADS, PEER_TOPK * PEER_TOPK)
        cidx = (si[:, :, 0, :, None] * N_KEYS + si[:, :, 1, None, :]).reshape(T, PEER_HEADS, PEER_TOPK * PEER_TOPK)
        top_s, top_i = lax.top_k(cand, PEER_TOPK)
        expert = jnp.take_along_axis(cidx, top_i, axis=-1)
        gate = jax.nn.softmax(top_s, axis=-1)
        act = jax.nn.gelu(jnp.einsum('thkd,td->thk', u_tab[expert], xb))
        return jnp.einsum('thk,thkd->td', (gate * act).astype(xb.dtype), v_tab[expert])

    return lax.map(block, xt).reshape(B, L, D)


def token_mixers(h, ctx, rope, lam_init, w_in, sgu_norm_g, sgu_w, sgu_b, lq1, lk1, lq2, lk2, subln_g, qn_g, kn_g, w_out):
    B, L, _ = h.shape
    splits = np.cumsum(IN_SIZES)[:-1].tolist()
    a_u, a_v, d_q, d_k, d_v, g_q, g_k, g_v = jnp.split(h @ w_in, splits, axis=-1)

    a_u = jax.nn.gelu(a_u)
    a_v = rms_norm(jax.nn.gelu(a_v), sgu_norm_g).reshape(B, L // SGU_CHUNK, SGU_CHUNK, SGU_GROUPS, HEAD_DIM)
    a_mix = jnp.einsum('gpq,bnqgc->bnpgc', sgu_w, a_v) + sgu_b.T[:, :, None]
    out_a = a_u * a_mix.reshape(B, L, SGU_W)

    d_q = d_q.reshape(B, L, DIFF_HEADS, 2, DIFF_DQK)
    d_k = d_k.reshape(B, L, DIFF_HEADS, 2, DIFF_DQK)
    d_v = d_v.reshape(B, L, DIFF_HEADS, DIFF_DV)
    g_q = rms_norm(g_q.reshape(B, L, GQA_HEADS, HEAD_DIM), qn_g)
    g_k = rms_norm(g_k.reshape(B, L, GQA_KV_HEADS, HEAD_DIM), kn_g)
    g_v = g_v.reshape(B, L, GQA_KV_HEADS, HEAD_DIM)
    ctx_out = (d_k, d_v, g_k, g_v)
    if ctx is None:
        dk_all, dv_all, gk_all, gv_all = ctx_out
    else:
        (cos_d, sin_d), (cos_g, sin_g) = rope
        c_dk, c_dv, c_gk, c_gv = ctx
        d_q = apply_rope(d_q, cos_d, sin_d)
        g_q = apply_rope(g_q, cos_g, sin_g)
        dk_all = jnp.concatenate([apply_rope(d_k, cos_d, sin_d), c_dk.astype(d_k.dtype)], axis=1)
        dv_all = jnp.concatenate([d_v, c_dv.astype(d_v.dtype)], axis=1)
        gk_all = jnp.concatenate([apply_rope(g_k, cos_g, sin_g), c_gk.astype(g_k.dtype)], axis=1)
        gv_all = jnp.concatenate([g_v, c_gv.astype(g_v.dtype)], axis=1)

    f32 = jnp.float32
    lam = (jnp.exp(jnp.sum(lq1.astype(f32) * lk1.astype(f32)))
           - jnp.exp(jnp.sum(lq2.astype(f32) * lk2.astype(f32))) + lam_init)

    def diff_block(qb):
        s = jnp.einsum('bqhxd,bkhxd->bhxqk', qb, dk_all).astype(f32) * DIFF_SCALE
        p = jax.nn.softmax(s, axis=-1)
        w = p[:, :, 0] - lam * p[:, :, 1]
        return jnp.einsum('bhqk,bkhd->bqhd', w.astype(dv_all.dtype), dv_all)

    out_b = sweep_query_blocks(diff_block, d_q)
    out_b = (rms_norm(out_b, subln_g) * (1.0 - lam_init)).reshape(B, L, DIFF_HEADS * DIFF_DV)

    g_q = g_q.reshape(B, L, GQA_KV_HEADS, GQA_GROUP, HEAD_DIM)

    def gqa_block(qb):
        s = jnp.einsum('bqhgd,bkhd->bhgqk', qb, gk_all).astype(f32) * GQA_SCALE
        p = jax.nn.softmax(s, axis=-1)
        return jnp.einsum('bhgqk,bkhd->bqhgd', p.astype(gv_all.dtype), gv_all)

    out_c = sweep_query_blocks(gqa_block, g_q).reshape(B, L, GQA_HEADS * HEAD_DIM)

    out = jnp.concatenate([out_a, out_b, out_c], axis=-1) @ w_out
    return out, ctx_out


def run_trunk(x, cvec, ctx_cache, rope, p):
    new_ctx = []
    for l in range(DEPTH):
        lam_init = 0.8 - 0.6 * math.exp(-0.3 * l)
        mod = jax.nn.silu(cvec) @ p['ada_w'][l] + p['ada_b'][l]
        sh1, sc1, g1, sh2, sc2, g2 = jnp.split(mod[:, None, :], ADA_CHUNKS, axis=-1)
        h = rms_norm(x, p['norm1_g'][l]) * (1.0 + sc1) + sh1
        ctx_l = None if ctx_cache is None else tuple(t[:, l] for t in ctx_cache)
        mix, kv = token_mixers(h, ctx_l, rope, lam_init, p['w_in'][l], p['sgu_norm_g'][l], p['sgu_w'][l], p['sgu_b'][l],
                               p['diff_lq1'][l], p['diff_lk1'][l], p['diff_lq2'][l], p['diff_lk2'][l], p['diff_subln_g'][l],
                               p['gqa_qnorm_g'][l], p['gqa_knorm_g'][l], p['w_out'][l])
        x = x + g1 * mix
        h = rms_norm(x, p['norm2_g'][l]) * (1.0 + sc2) + sh2
        x = x + g2 * peer_ffn(h, p['peer_wq'][l], p['peer_subkeys'][l], p['peer_u'][l], p['peer_v'][l])
        if ctx_cache is None:
            new_ctx.append(kv)
    return rms_norm(x, p['final_g']), new_ctx


def setup_inputs(seed: int = 0) -> dict:
    key = jax.random.key(seed)
    ks = list(jax.random.split(key, 32))

    def nrm(i, shape, scale=1.0):
        return scale * jax.random.normal(ks[i], shape, jnp.float32)

    def gain(i, shape):
        return 1.0 + 0.01 * jax.random.normal(ks[i], shape, jnp.float32)

    D = D_MODEL
    return {
        'x_prompt': nrm(0, (BATCH, SEQ, D)),
        'x_sample': nrm(1, (DEC_BATCH, DEC_SEQ, D)),
        'cache_diff_k': nrm(2, (DEC_BATCH, DEPTH, PAST_LEN, DIFF_HEADS, 2, DIFF_DQK)),
        'cache_diff_v': nrm(3, (DEC_BATCH, DEPTH, PAST_LEN, DIFF_HEADS, DIFF_DV)),
        'cache_gqa_k': nrm(4, (DEC_BATCH, DEPTH, PAST_LEN, GQA_KV_HEADS, HEAD_DIM)),
        'cache_gqa_v': nrm(5, (DEC_BATCH, DEPTH, PAST_LEN, GQA_KV_HEADS, HEAD_DIM)),
        'c': nrm(6, (DEC_BATCH, D)),
        'c_ctx': nrm(7, (D,)),
        'ada_w': nrm(8, (DEPTH, D, ADA_CHUNKS * D), 0.5 * D ** -0.5),
        'ada_b': nrm(9, (DEPTH, ADA_CHUNKS * D), 0.01),
        'norm1_g': gain(10, (DEPTH, D)),
        'norm2_g': gain(11, (DEPTH, D)),
        'w_in': nrm(12, (DEPTH, D, IN_W), D ** -0.5),
        'sgu_norm_g': gain(13, (DEPTH, SGU_W)),
        'sgu_w': nrm(14, (DEPTH, SGU_GROUPS, SGU_CHUNK, SGU_CHUNK), SGU_CHUNK ** -0.5),
        'sgu_b': gain(15, (DEPTH, SGU_GROUPS, SGU_CHUNK)),
        'diff_lq1': nrm(16, (DEPTH, DIFF_DQK), 0.1),
        'diff_lk1': nrm(17, (DEPTH, DIFF_DQK), 0.1),
        'diff_lq2': nrm(18, (DEPTH, DIFF_DQK), 0.1),
        'diff_lk2': nrm(19, (DEPTH, DIFF_DQK), 0.1),
        'diff_subln_g': gain(20, (DEPTH, DIFF_DV)),
        'gqa_qnorm_g': gain(21, (DEPTH, HEAD_DIM)),
        'gqa_knorm_g': gain(22, (DEPTH, HEAD_DIM)),
        'w_out': nrm(23, (DEPTH, MIX_W, D), MIX_W ** -0.5),
        'peer_wq': nrm(24, (DEPTH, D, PEER_HEADS * PEER_QDIM), D ** -0.5),
        'peer_subkeys': nrm(25, (DEPTH, 2, N_KEYS, PEER_HALF), PEER_HALF ** -0.5),
        'peer_u': nrm(26, (DEPTH, N_EXPERTS, D), D ** -0.5),
        'peer_v': nrm(27, (DEPTH, N_EXPERTS, D), 0.1),
        'final_g': gain(28, (D,)),
    }


def reference(x_prompt, x_sample, cache_diff_k, cache_diff_v, cache_gqa_k, cache_gqa_v, c, c_ctx,
              ada_w, ada_b, norm1_g, norm2_g, w_in, sgu_norm_g, sgu_w, sgu_b,
              diff_lq1, diff_lk1, diff_lq2, diff_lk2, diff_subln_g, gqa_qnorm_g, gqa_knorm_g, w_out,
              peer_wq, peer_subkeys, peer_u, peer_v, final_g):
    p = {'ada_w': ada_w, 'ada_b': ada_b, 'norm1_g': norm1_g, 'norm2_g': norm2_g, 'w_in': w_in,
         'sgu_norm_g': sgu_norm_g, 'sgu_w': sgu_w, 'sgu_b': sgu_b,
         'diff_lq1': diff_lq1, 'diff_lk1': diff_lk1, 'diff_lq2': diff_lq2, 'diff_lk2': diff_lk2,
         'diff_subln_g': diff_subln_g, 'gqa_qnorm_g': gqa_qnorm_g, 'gqa_knorm_g': gqa_knorm_g, 'w_out': w_out,
         'peer_wq': peer_wq, 'peer_subkeys': peer_subkeys, 'peer_u': peer_u, 'peer_v': peer_v, 'final_g': final_g}

    c_prompt = jnp.broadcast_to(c_ctx, (x_prompt.shape[0], c_ctx.shape[0]))
    y_prompt, ctx_layers = run_trunk(x_prompt, c_prompt, None, None, p)
    new_diff_k = jnp.stack([kv[0] for kv in ctx_layers], axis=1)
    new_diff_v = jnp.stack([kv[1] for kv in ctx_layers], axis=1)
    new_gqa_k = jnp.stack([kv[2] for kv in ctx_layers], axis=1)
    new_gqa_v = jnp.stack([kv[3] for kv in ctx_layers], axis=1)

    n_lat = x_sample.shape[1]
    rope = (axial_rope(n_lat, DIFF_DQK), axial_rope(n_lat, HEAD_DIM))
    y_sample, _ = run_trunk(x_sample, c, (cache_diff_k, cache_diff_v, cache_gqa_k, cache_gqa_v), rope, p)

    return (y_prompt, y_sample, new_diff_k, new_diff_v, new_gqa_k, new_gqa_v)
```

```python
import functools
import math

import jax
import jax.numpy as jnp
from jax import lax
from jax.experimental import pallas as pl
from jax.experimental.pallas import tpu as pltpu

F32 = jnp.float32
BF16 = jnp.bfloat16

HEAD_DIM = 64
SGU_CHUNK = 128
DIFF_DQK = 32
GQA_GROUP = 4
ADA_CHUNKS = 6
N_KEYS = 128
PEER_HEADS = 8
PEER_TOPK = 16
PEER_HALF = 64
GRID_W = 64
ROPE_THETA = 10000.0
EPS = 1e-6
DIFF_SCALE = DIFF_DQK ** -0.5
GQA_SCALE = HEAD_DIM ** -0.5

LANES = 128
VMEM_LIMIT_BYTES = 56 * 1024 * 1024

FRONT_ROWS = 256
ATTN_ROWS = 256
PEER_ROWS = 512
PEER_KEY_ROWS = 8


def _dot(a, b):
    return jnp.dot(a, b, preferred_element_type=F32)


def _dot_nt(a, b):
    return lax.dot_general(a, b, (((1,), (1,)), ((), ())), preferred_element_type=F32)


def _rms(x):
    return x * lax.rsqrt(jnp.mean(x * x, axis=-1, keepdims=True) + EPS)


def _gelu(x):
    c = math.sqrt(2.0 / math.pi)
    return 0.5 * x * (1.0 + jnp.tanh(c * (x + 0.044715 * (x * x * x))))


def _params(n_axes):
    return pltpu.CompilerParams(
        dimension_semantics=("arbitrary",) * n_axes, vmem_limit_bytes=VMEM_LIMIT_BYTES)


def _mod_kernel(c_ref, w_ref, b_ref, o_ref):
    c = c_ref[...]
    s = c / (1.0 + jnp.exp(-c))
    o_ref[0] = _dot(s.astype(BF16), w_ref[0].astype(BF16)) + b_ref[0]


def _modulation(cvecs, ada_w, ada_b):
    depth, d, n = ada_w.shape
    rows = cvecs.shape[0]
    tn = n // 4
    return pl.pallas_call(
        _mod_kernel,
        grid=(depth, n // tn),
        in_specs=[
            pl.BlockSpec((rows, d), lambda l, j: (0, 0)),
            pl.BlockSpec((1, d, tn), lambda l, j: (l, 0, j)),
            pl.BlockSpec((1, 1, tn), lambda l, j: (l, 0, j)),
        ],
        out_specs=pl.BlockSpec((1, rows, tn), lambda l, j: (l, 0, j)),
        out_shape=jax.ShapeDtypeStruct((depth, rows, n), F32),
        compiler_params=_params(2),
        name="modulation",
    )(cvecs, ada_w, ada_b.reshape(depth, 1, n))


def _rope(x, cos, sin_signed, half):
    lane = lax.broadcasted_iota(jnp.int32, (1, LANES), 1)
    first = (lane % (2 * half)) < half
    partner = jnp.where(first, pltpu.roll(x, LANES - half, 1), pltpu.roll(x, half, 1))
    return x * cos + partner * sin_signed


def _front_kernel(*refs, rope, emit_kv, d_model, sgu_w_cols):
    (x_ref, mod_ref, g1_ref, win_ref, sgn_ref, sgw_ref, sgb_ref, qng_ref, kng_ref) = refs[:9]
    pos = 9
    if rope:
        cd_ref, sd_ref, cg_ref, sg_ref = refs[pos:pos + 4]
        pos += 4
    oa_ref, dq_ref, dk_ref, dv_ref, gq_ref, gk_ref, gv_ref = refs[pos:pos + 7]
    pos += 7
    if emit_kv:
        kd32_ref, vd32_ref, kg32_ref, vg32_ref = refs[pos:pos + 4]

    x = x_ref[...]
    m = mod_ref[0]
    h = _rms(x) * g1_ref[...] * (1.0 + m[1:2]) + m[0:1]
    proj = _dot(h.astype(BF16), win_ref[...])

    w = sgu_w_cols
    a_u = _gelu(proj[:, 0:w])
    a_v = _rms(_gelu(proj[:, w:2 * w])) * sgn_ref[...]
    a_vb = a_v.astype(BF16)
    lane_w = lax.broadcasted_iota(jnp.int32, (1, w), 1)
    n_groups = w // HEAD_DIM
    t = x.shape[0]
    for ch in range(t // SGU_CHUNK):
        rows = slice(ch * SGU_CHUNK, (ch + 1) * SGU_CHUNK)
        av = a_vb[rows]
        mix = jnp.zeros((SGU_CHUNK, w), F32)
        for g in range(n_groups):
            part = _dot(sgw_ref[g], av)
            mix = jnp.where(lane_w // HEAD_DIM == g, part, mix)
        oa_ref[rows, :] = (a_u[rows] * (mix + sgb_ref[...])).astype(BF16)

    o = 2 * w
    d_q = proj[:, o:o + 256]
    d_k = proj[:, o + 256:o + 512]
    d_v = proj[:, o + 512:o + 768]
    g_q = proj[:, o + 768:o + 1792]
    g_k = proj[:, o + 1792:o + 1920]
    g_v = proj[:, o + 1920:o + 2048]

    lane = lax.broadcasted_iota(jnp.int32, (1, LANES), 1)
    low = lane < HEAD_DIM
    k2 = g_k * g_k
    s_low = jnp.sum(jnp.where(low, k2, 0.0), axis=-1, keepdims=True)
    s_all = jnp.sum(k2, axis=-1, keepdims=True)
    ms = jnp.where(low, s_low, s_all - s_low) * (1.0 / HEAD_DIM)
    g_k = g_k * lax.rsqrt(ms + EPS) * kng_ref[...]

    if emit_kv:
        kd32_ref[...] = d_k
        vd32_ref[...] = d_v
        kg32_ref[...] = g_k
        vg32_ref[...] = g_v

    if rope:
        g_k = _rope(g_k, cg_ref[...], sg_ref[...], HEAD_DIM // 2)
    gk_ref[...] = g_k.astype(BF16)
    gv_ref[...] = g_v.astype(BF16)
    dv_ref[...] = d_v.astype(BF16)

    for s in range(2):
        cols = slice(s * LANES, (s + 1) * LANES)
        q = d_q[:, cols]
        k = d_k[:, cols]
        if rope:
            q = _rope(q, cd_ref[...], sd_ref[...], DIFF_DQK // 2)
            k = _rope(k, cd_ref[...], sd_ref[...], DIFF_DQK // 2)
        dq_ref[:, cols] = (q * DIFF_SCALE).astype(BF16)
        dk_ref[:, cols] = k.astype(BF16)

    for s in range(g_q.shape[1] // LANES):
        cols = slice(s * LANES, (s + 1) * LANES)
        q = g_q[:, cols]
        ms = jnp.sum(q * q, axis=-1, keepdims=True) * (1.0 / HEAD_DIM)
        q = q * lax.rsqrt(ms + EPS) * qng_ref[...]
        if rope:
            q = _rope(q, cg_ref[...], sg_ref[...], HEAD_DIM // 2)
        gq_ref[:, cols] = (q * GQA_SCALE).astype(BF16)


def _front(x, mod_l, row_of_block, g1, win, sgn, sgw, sgb, qng, kng, rope_tabs, emit_kv):
    n, d = x.shape
    t = FRONT_ROWS
    nb = n // t
    rope = rope_tabs is not None
    const2 = lambda i: (0, 0)
    in_specs = [
        pl.BlockSpec((t, d), lambda i: (i, 0)),
        pl.BlockSpec((1, ADA_CHUNKS, d), lambda i: (row_of_block(i), 0, 0)),
        pl.BlockSpec((1, d), const2),
        pl.BlockSpec(win.shape, const2),
        pl.BlockSpec(sgn.shape, const2),
        pl.BlockSpec(sgw.shape, lambda i: (0, 0, 0)),
        pl.BlockSpec(sgb.shape, const2),
        pl.BlockSpec(qng.shape, const2),
        pl.BlockSpec(kng.shape, const2),
    ]
    args = [x, mod_l, g1, win, sgn, sgw, sgb, qng, kng]
    if rope:
        blocks_per_seq = rope_tabs[0].shape[0] // t
        for tab in rope_tabs:
            in_specs.append(pl.BlockSpec((t, LANES), lambda i: (i % blocks_per_seq, 0)))
            args.append(tab)
    widths = [(256, BF16), (256, BF16), (256, BF16), (256, BF16), (1024, BF16), (128, BF16), (128, BF16)]
    if emit_kv:
        widths += [(256, F32), (256, F32), (128, F32), (128, F32)]
    out_specs = [pl.BlockSpec((t, wd), lambda i: (i, 0)) for wd, _ in widths]
    out_shape = [jax.ShapeDtypeStruct((n, wd), dt) for wd, dt in widths]
    return pl.pallas_call(
        functools.partial(_front_kernel, rope=rope, emit_kv=emit_kv, d_model=d, sgu_w_cols=sgn.shape[1]),
        grid=(nb,),
        in_specs=in_specs,
        out_specs=out_specs,
        out_shape=out_shape,
        compiler_params=_params(1),
        name="front",
    )(*args)


def _softmax_pv(s, v):
    mx = jnp.max(s, axis=-1, keepdims=True)
    e = jnp.exp(s - mx)
    z = jnp.sum(e, axis=-1, keepdims=True)
    return _dot(e.astype(BF16), v) * (1.0 / z)


def _attn_kernel(x_ref, mod_ref, oa_ref, dq_ref, gq_ref, dk_ref, dv_ref, gk_ref, gv_ref,
                 lam_ref, subln_ref, woab_ref, woc_ref, out_ref, oc_ref, *, lam_init):
    lam = (jnp.exp(jnp.sum(lam_ref[0:1] * lam_ref[1:2], axis=-1, keepdims=True))
           - jnp.exp(jnp.sum(lam_ref[2:3] * lam_ref[3:4], axis=-1, keepdims=True)) + lam_init)

    dq = dq_ref[...]
    dk = dk_ref[...]
    dv = dv_ref[...]
    w = dq.shape[1]
    lane = lax.broadcasted_iota(jnp.int32, (1, w), 1)
    out_b = jnp.zeros(dq.shape, F32)
    inv = jnp.zeros(dq.shape, F32)
    n_heads = w // HEAD_DIM
    for hd in range(n_heads):
        parts = []
        for xx in range(2):
            sel = (lane // DIFF_DQK) == (2 * hd + xx)
            qm = jnp.where(sel, dq, jnp.zeros_like(dq))
            parts.append(_softmax_pv(_dot_nt(qm, dk), dv))
        o = parts[0] - lam * parts[1]
        in_head = (lane // HEAD_DIM) == hd
        ms = jnp.sum(jnp.where(in_head, o * o, 0.0), axis=-1, keepdims=True) * (1.0 / HEAD_DIM)
        out_b = jnp.where(in_head, o, out_b)
        inv = jnp.where(in_head, lax.rsqrt(ms + EPS), inv)
    out_b = out_b * inv * subln_ref[...] * (1.0 - lam_init)

    gk = gk_ref[...]
    gv = gv_ref[...]
    for hq in range(gq_ref.shape[1] // LANES):
        cols = slice(hq * LANES, (hq + 1) * LANES)
        oc_ref[:, cols] = _softmax_pv(_dot_nt(gq_ref[:, cols], gk), gv).astype(BF16)

    ab = jnp.concatenate([oa_ref[...], out_b.astype(BF16)], axis=-1)
    mix = _dot(ab, woab_ref[...]) + _dot(oc_ref[...], woc_ref[...])
    out_ref[...] = x_ref[...] + mod_ref[0][2:3] * mix


def _attn(x3, mod_l, row_of_batch, oa, dq, gq, dk, dv, gk, gv, lam_vecs, subln, woab, woc, lam_init):
    b, l, d = x3.shape
    tq = ATTN_ROWS
    lk = dk.shape[1]
    qspec = lambda wd: pl.BlockSpec((None, tq, wd), lambda i, j: (i, j, 0))
    kspec = lambda wd: pl.BlockSpec((None, lk, wd), lambda i, j: (i, 0, 0))
    const2 = lambda i, j: (0, 0)
    return pl.pallas_call(
        functools.partial(_attn_kernel, lam_init=lam_init),
        grid=(b, l // tq),
        in_specs=[
            qspec(d),
            pl.BlockSpec((1, ADA_CHUNKS, d), lambda i, j: (row_of_batch(i), 0, 0)),
            qspec(oa.shape[2]), qspec(dq.shape[2]), qspec(gq.shape[2]),
            kspec(dk.shape[2]), kspec(dv.shape[2]), kspec(gk.shape[2]), kspec(gv.shape[2]),
            pl.BlockSpec(lam_vecs.shape, const2),
            pl.BlockSpec(subln.shape, const2),
            pl.BlockSpec(woab.shape, const2),
            pl.BlockSpec(woc.shape, const2),
        ],
        out_specs=qspec(d),
        out_shape=jax.ShapeDtypeStruct((b, l, d), F32),
        scratch_shapes=[pltpu.VMEM((tq, gq.shape[2]), BF16)],
        compiler_params=_params(2),
        name="attn",
    )(x3, mod_l, oa, dq, gq, dk, dv, gk, gv, lam_vecs, subln, woab, woc)


def _top_rows(s, k, rows_ref):
    cur = s
    for r in range(k):
        mx = jnp.max(cur, axis=0, keepdims=True)
        rows_ref[r:r + 1, :] = mx
        if r + 1 < k:
            cur = jnp.where(cur == mx, -jnp.inf, cur)


def _kth_largest(c, k):
    cur = c
    for _ in range(k - 1):
        mx = jnp.max(cur, axis=0, keepdims=True)
        cur = jnp.where(cur == mx, -jnp.inf, cur)
    return jnp.max(cur, axis=0, keepdims=True)


def _peer_kernel(x_ref, mod_ref, g2_ref, wqt_ref, sk_ref, u_ref, vt_ref, fin_ref, out_ref,
                 hb_ref, s2_ref, bb_ref, cc_ref, aa_ref, v1_ref, v2_ref, pre_ref, wt_ref, acc_ref,
                 *, final_norm):
    c = pl.program_id(1)
    t = x_ref.shape[0]

    @pl.when(c == 0)
    def _routing():
        m = mod_ref[0]
        h = _rms(x_ref[...]) * g2_ref[...] * (1.0 + m[4:5]) + m[3:4]
        hb = h.astype(BF16)
        hb_ref[...] = hb
        qt = _dot_nt(wqt_ref[...], hb)
        for hd in range(PEER_HEADS):
            base = hd * 2 * PEER_HALF
            s1 = _dot(sk_ref[0], qt[base:base + PEER_HALF].astype(BF16))
            s2 = _dot(sk_ref[1], qt[base + PEER_HALF:base + 2 * PEER_HALF].astype(BF16))
            _top_rows(s1, PEER_TOPK, v1_ref)
            _top_rows(s2, PEER_TOPK, v2_ref)
            v1 = v1_ref[...]
            v2 = v2_ref[...]
            cands = [v1[0:1] + v2]
            cands += [v1[r:r + 1] + v2[0:8] for r in range(1, PEER_TOPK)]
            cand = jnp.concatenate(cands, axis=0)
            tau = _kth_largest(cand, PEER_TOPK)
            top = v1[0:1] + v2[0:1]
            z = jnp.sum(jnp.where(cand >= tau, jnp.exp(cand - top), 0.0), axis=0, keepdims=True)
            s2_ref[hd] = s2
            bb_ref[hd] = jnp.exp(s2 - v2[0:1])
            thr = tau - s1
            a = jnp.exp(s1 - v1[0:1]) * (1.0 / z)
            for k in range(N_KEYS // PEER_KEY_ROWS):
                cc_ref[hd, k] = thr[k * PEER_KEY_ROWS:(k + 1) * PEER_KEY_ROWS]
                aa_ref[hd, k] = a[k * PEER_KEY_ROWS:(k + 1) * PEER_KEY_ROWS]
        acc_ref[...] = jnp.zeros(acc_ref.shape, F32)

    pre_ref[...] = _dot_nt(u_ref[...], hb_ref[...])

    for r in range(PEER_KEY_ROWS):
        rows = slice(r * N_KEYS, (r + 1) * N_KEYS)
        for lt in range(t // LANES):
            cols = slice(lt * LANES, (lt + 1) * LANES)
            g = jnp.zeros((N_KEYS, LANES), F32)
            for hd in range(PEER_HEADS):
                thr = cc_ref[hd, c, r:r + 1, cols]
                a = aa_ref[hd, c, r:r + 1, cols]
                g = g + jnp.where(s2_ref[hd, :, cols] >= thr, bb_ref[hd, :, cols], 0.0) * a
            wt_ref[rows, cols] = (_gelu(pre_ref[rows, cols]) * g).astype(BF16)
    acc_ref[...] += _dot(vt_ref[...], wt_ref[...])

    @pl.when(c == pl.num_programs(1) - 1)
    def _finish():
        y = x_ref[...] + mod_ref[0][5:6] * acc_ref[...].T
        if final_norm:
            y = _rms(y) * fin_ref[...]
        out_ref[...] = y


def _peer(x, mod_l, row_of_block, g2, wqt, sk, u, vt, fin, final_norm):
    n, d = x.shape
    t = PEER_ROWS
    ec = PEER_KEY_ROWS * N_KEYS
    n_exp = u.shape[0]
    const2 = lambda i, c: (0, 0)
    gate = pltpu.VMEM((PEER_HEADS, N_KEYS, t), F32)
    gate_rows = pltpu.VMEM((PEER_HEADS, N_KEYS // PEER_KEY_ROWS, PEER_KEY_ROWS, t), F32)
    return pl.pallas_call(
        functools.partial(_peer_kernel, final_norm=final_norm),
        grid=(n // t, n_exp // ec),
        in_specs=[
            pl.BlockSpec((t, d), lambda i, c: (i, 0)),
            pl.BlockSpec((1, ADA_CHUNKS, d), lambda i, c: (row_of_block(i), 0, 0)),
            pl.BlockSpec((1, d), const2),
            pl.BlockSpec(wqt.shape, const2),
            pl.BlockSpec(sk.shape, lambda i, c: (0, 0, 0)),
            pl.BlockSpec((ec, d), lambda i, c: (c, 0)),
            pl.BlockSpec((d, ec), lambda i, c: (0, c)),
            pl.BlockSpec((1, d), const2),
        ],
        out_specs=pl.BlockSpec((t, d), lambda i, c: (i, 0)),
        out_shape=jax.ShapeDtypeStruct((n, d), F32),
        scratch_shapes=[
            pltpu.VMEM((t, d), BF16),
            gate, gate, gate_rows, gate_rows,
            pltpu.VMEM((PEER_TOPK, t), F32),
            pltpu.VMEM((PEER_TOPK, t), F32),
            pltpu.VMEM((ec, t), F32),
            pltpu.VMEM((ec, t), BF16),
            pltpu.VMEM((d, t), F32),
        ],
        compiler_params=_params(2),
        name="peer",
    )(x, mod_l, g2, wqt, sk, u, vt, fin)


def _rope_tables(n_tokens, dim):
    rows = n_tokens // GRID_W
    row = jnp.repeat(jnp.arange(rows, dtype=F32), GRID_W)
    col = jnp.tile(jnp.arange(GRID_W, dtype=F32), rows)
    quarter = dim // 4
    inv = ROPE_THETA ** (-jnp.arange(quarter, dtype=F32) / quarter)
    ang = jnp.concatenate([row[:, None] * inv, col[:, None] * inv], axis=-1)
    cos, sin = jnp.cos(ang), jnp.sin(ang)
    reps = LANES // dim
    return (jnp.tile(jnp.concatenate([cos, cos], axis=-1), (1, reps)),
            jnp.tile(jnp.concatenate([-sin, sin], axis=-1), (1, reps)))


def _slot_pad(a, axis, n_heads):
    parts = []
    for hq in range(n_heads):
        blk = lax.slice_in_dim(a, hq * HEAD_DIM, (hq + 1) * HEAD_DIM, axis=axis)
        zero = jnp.zeros_like(blk)
        parts += [blk, zero] if (hq // GQA_GROUP) == 0 else [zero, blk]
    return jnp.concatenate(parts, axis=axis)


def kernel(x_prompt, x_sample, cache_diff_k, cache_diff_v, cache_gqa_k, cache_gqa_v, c, c_ctx, ada_w, ada_b, norm1_g, norm2_g, w_in, sgu_norm_g, sgu_w, sgu_b, diff_lq1, diff_lk1, diff_lq2, diff_lk2, diff_subln_g, gqa_qnorm_g, gqa_knorm_g, w_out, peer_wq, peer_subkeys, peer_u, peer_v, final_g):
    depth, d, _ = w_in.shape
    bp, lp, _ = x_prompt.shape
    bs, ls, _ = x_sample.shape
    past = cache_diff_k.shape[2]
    sgu_cols = sgu_norm_g.shape[1]
    diff_cols = diff_subln_g.shape[1] * cache_diff_v.shape[3]
    kv_cols = cache_gqa_k.shape[3] * HEAD_DIM
    gq_cols = kv_cols * GQA_GROUP
    n_gqa_heads = gq_cols // HEAD_DIM
    assert kv_cols == LANES and diff_cols == 2 * LANES and sgu_cols == 2 * LANES

    n_rows = -(-(1 + bs) // 8) * 8
    cvecs = jnp.zeros((n_rows, d), F32).at[0].set(c_ctx).at[1:1 + bs].set(c)
    mod = _modulation(cvecs, ada_w, ada_b).reshape(depth, n_rows, ADA_CHUNKS, d)

    o = 2 * sgu_cols + 3 * diff_cols
    win_ext = jnp.concatenate(
        [w_in[:, :, :o], _slot_pad(w_in[:, :, o:o + gq_cols], 2, n_gqa_heads), w_in[:, :, o + gq_cols:]],
        axis=2).astype(BF16)
    woab = w_out[:, :sgu_cols + diff_cols].astype(BF16)
    woc = _slot_pad(w_out[:, sgu_cols + diff_cols:], 1, n_gqa_heads).astype(BF16)
    sgw = sgu_w.astype(BF16)
    sgb = jnp.repeat(jnp.swapaxes(sgu_b, 1, 2), HEAD_DIM, axis=2)
    qng = jnp.tile(gqa_qnorm_g, (1, 2))[:, None, :]
    kng = jnp.tile(gqa_knorm_g, (1, 2))[:, None, :]
    subln = jnp.tile(diff_subln_g, (1, diff_cols // HEAD_DIM))[:, None, :]
    lam_vecs = jnp.stack([diff_lq1, diff_lk1, diff_lq2, diff_lk2], axis=1)
    wqt = jnp.swapaxes(peer_wq, 1, 2).astype(BF16)
    sk = peer_subkeys.astype(BF16)
    u_tab = peer_u.astype(BF16)
    vt_tab = jnp.swapaxes(peer_v, 1, 2).astype(BF16)
    tabs = _rope_tables(ls, DIFF_DQK) + _rope_tables(ls, HEAD_DIM)

    def run(x3, sample):
        b, l, _ = x3.shape
        x = x3.reshape(b * l, d)
        if sample:
            row_front = lambda i: 1 + i // (l // FRONT_ROWS)
            row_attn = lambda i: 1 + i
            row_peer = lambda i: 1 + i // (l // PEER_ROWS)
        else:
            row_front = row_attn = row_peer = lambda i: 0
        kv_out = []
        for layer in range(depth):
            lam_init = 0.8 - 0.6 * math.exp(-0.3 * layer)
            mod_l = mod[layer]
            outs = _front(x, mod_l, row_front, norm1_g[layer][None], win_ext[layer], sgu_norm_g[layer][None],
                          sgw[layer], sgb[layer], qng[layer], kng[layer], tabs if sample else None, not sample)
            oa, dq, dk, dv, gq, gk, gv = [a.reshape(b, l, a.shape[1]) for a in outs[:7]]
            if sample:
                cat = lambda new, old: jnp.concatenate(
                    [new, old[:, layer].reshape(b, past, -1).astype(BF16)], axis=1)
                dk, dv = cat(dk, cache_diff_k), cat(dv, cache_diff_v)
                gk, gv = cat(gk, cache_gqa_k), cat(gv, cache_gqa_v)
            else:
                kv_out.append(outs[7:])
            x3n = _attn(x.reshape(b, l, d), mod_l, row_attn, oa, dq, gq, dk, dv, gk, gv,
                        lam_vecs[layer], subln[layer], woab[layer], woc[layer], lam_init)
            x = _peer(x3n.reshape(b * l, d), mod_l, row_peer, norm2_g[layer][None], wqt[layer], sk[layer],
                      u_tab[layer], vt_tab[layer], final_g[None], layer == depth - 1)
        return x.reshape(b, l, d), kv_out

    y_prompt, kv = run(x_prompt, False)
    y_sample, _ = run(x_sample, True)
    stack = lambda idx, shape: jnp.stack([kv[layer][idx].reshape(shape) for layer in range(depth)], axis=1)
    new_diff_k = stack(0, (bp, lp) + cache_diff_k.shape[3:])
    new_diff_v = stack(1, (bp, lp) + cache_diff_v.shape[3:])
    new_gqa_k = stack(2, (bp, lp) + cache_gqa_k.shape[3:])
    new_gqa_v = stack(3, (bp, lp) + cache_gqa_v.shape[3:])
    return (y_prompt, y_sample, new_diff_k, new_diff_v, new_gqa_k, new_gqa_v)
```

```python
import functools
import math

import jax
import jax.numpy as jnp
from jax import lax
from jax.experimental import pallas as pl
from jax.experimental.pallas import tpu as pltpu

F32 = jnp.float32
BF16 = jnp.bfloat16

HEAD_DIM = 64
SGU_CHUNK = 128
DIFF_DQK = 32
GQA_GROUP = 4
ADA_CHUNKS = 6
N_KEYS = 128
PEER_HEADS = 8
PEER_TOPK = 16
PEER_HALF = 64
GRID_W = 64
ROPE_THETA = 10000.0
EPS = 1e-6
DIFF_SCALE = DIFF_DQK ** -0.5
GQA_SCALE = HEAD_DIM ** -0.5

LANES = 128
BF16_ROWS = 16
VMEM_LIMIT_BYTES = 56 * 1024 * 1024

FRONT_ROWS = 256
ATTN_ROWS = 256
PEER_ROWS = 512
PEER_KEY_ROWS = 8


def _dot(a, b):
    return jnp.dot(a, b, preferred_element_type=F32)


def _dot_nt(a, b):
    return lax.dot_general(a, b, (((1,), (1,)), ((), ())), preferred_element_type=F32)


def _rms(x):
    return x * lax.rsqrt(jnp.mean(x * x, axis=-1, keepdims=True) + EPS)


def _gelu(x):
    c = math.sqrt(2.0 / math.pi)
    return 0.5 * x * (1.0 + jnp.tanh(c * (x + 0.044715 * (x * x * x))))


def _params(n_axes, flags=None):
    return pltpu.CompilerParams(
        dimension_semantics=("arbitrary",) * n_axes, vmem_limit_bytes=VMEM_LIMIT_BYTES, flags=flags)


def _mod_kernel(c_ref, w_ref, b_ref, o_ref):
    c = c_ref[...]
    s = c / (1.0 + jnp.exp(-c))
    o_ref[0] = _dot(s.astype(BF16), w_ref[0].astype(BF16)) + b_ref[0]


def _modulation(cvecs, ada_w, ada_b):
    depth, d, n = ada_w.shape
    rows = cvecs.shape[0]
    tn = n // 4
    return pl.pallas_call(
        _mod_kernel,
        grid=(depth, n // tn),
        in_specs=[
            pl.BlockSpec((rows, d), lambda l, j: (0, 0)),
            pl.BlockSpec((1, d, tn), lambda l, j: (l, 0, j)),
            pl.BlockSpec((1, 1, tn), lambda l, j: (l, 0, j)),
        ],
        out_specs=pl.BlockSpec((1, rows, tn), lambda l, j: (l, 0, j)),
        out_shape=jax.ShapeDtypeStruct((depth, rows, n), F32),
        compiler_params=_params(2),
        name="modulation",
    )(cvecs, ada_w, ada_b.reshape(depth, 1, n))


def _rope(x, cos, sin_signed, half):
    lane = lax.broadcasted_iota(jnp.int32, (1, LANES), 1)
    first = (lane % (2 * half)) < half
    partner = jnp.where(first, pltpu.roll(x, LANES - half, 1), pltpu.roll(x, half, 1))
    return x * cos + partner * sin_signed


def _front_kernel(*refs, rope, emit_kv, d_model, sgu_w_cols):
    (x_ref, mod_ref, g1_ref, win_ref, sgn_ref, sgw_ref, sgb_ref, qng_ref, kng_ref) = refs[:9]
    pos = 9
    if rope:
        cd_ref, sd_ref, cg_ref, sg_ref = refs[pos:pos + 4]
        pos += 4
    oa_ref, dq_ref, dk_ref, dv_ref, gq_ref, gk_ref, gv_ref = refs[pos:pos + 7]
    pos += 7
    if emit_kv:
        kd32_ref, vd32_ref, kg32_ref, vg32_ref = refs[pos:pos + 4]

    x = x_ref[...]
    m = mod_ref[0]
    h = _rms(x) * g1_ref[...] * (1.0 + m[1:2]) + m[0:1]
    proj = _dot(h.astype(BF16), win_ref[...])

    w = sgu_w_cols
    a_u = _gelu(proj[:, 0:w])
    a_v = _rms(_gelu(proj[:, w:2 * w])) * sgn_ref[...]
    a_vb = a_v.astype(BF16)
    lane_w = lax.broadcasted_iota(jnp.int32, (1, w), 1)
    n_groups = w // HEAD_DIM
    t = x.shape[0]
    for ch in range(t // SGU_CHUNK):
        rows = slice(ch * SGU_CHUNK, (ch + 1) * SGU_CHUNK)
        av = a_vb[rows]
        mix = jnp.zeros((SGU_CHUNK, w), F32)
        for g in range(n_groups):
            part = _dot(sgw_ref[g], av)
            mix = jnp.where(lane_w // HEAD_DIM == g, part, mix)
        oa_ref[rows, :] = (a_u[rows] * (mix + sgb_ref[...])).astype(BF16)

    o = 2 * w
    d_q = proj[:, o:o + 256]
    d_k = proj[:, o + 256:o + 512]
    d_v = proj[:, o + 512:o + 768]
    g_q = proj[:, o + 768:o + 1792]
    g_k = proj[:, o + 1792:o + 1920]
    g_v = proj[:, o + 1920:o + 2048]

    lane = lax.broadcasted_iota(jnp.int32, (1, LANES), 1)
    low = lane < HEAD_DIM
    k2 = g_k * g_k
    s_low = jnp.sum(jnp.where(low, k2, 0.0), axis=-1, keepdims=True)
    s_all = jnp.sum(k2, axis=-1, keepdims=True)
    ms = jnp.where(low, s_low, s_all - s_low) * (1.0 / HEAD_DIM)
    g_k = g_k * lax.rsqrt(ms + EPS) * kng_ref[...]

    if emit_kv:
        kd32_ref[...] = d_k
        vd32_ref[...] = d_v
        kg32_ref[...] = g_k
        vg32_ref[...] = g_v

    if rope:
        g_k = _rope(g_k, cg_ref[...], sg_ref[...], HEAD_DIM // 2)
    gk_ref[...] = g_k.astype(BF16)
    gv_ref[...] = g_v.astype(BF16)
    dv_ref[...] = d_v.astype(BF16)

    for s in range(2):
        cols = slice(s * LANES, (s + 1) * LANES)
        q = d_q[:, cols]
        k = d_k[:, cols]
        if rope:
            q = _rope(q, cd_ref[...], sd_ref[...], DIFF_DQK // 2)
            k = _rope(k, cd_ref[...], sd_ref[...], DIFF_DQK // 2)
        dq_ref[:, cols] = (q * DIFF_SCALE).astype(BF16)
        dk_ref[:, cols] = k.astype(BF16)

    for s in range(g_q.shape[1] // LANES):
        cols = slice(s * LANES, (s + 1) * LANES)
        q = g_q[:, cols]
        ms = jnp.sum(q * q, axis=-1, keepdims=True) * (1.0 / HEAD_DIM)
        q = q * lax.rsqrt(ms + EPS) * qng_ref[...]
        if rope:
            q = _rope(q, cg_ref[...], sg_ref[...], HEAD_DIM // 2)
        gq_ref[:, cols] = (q * GQA_SCALE).astype(BF16)


def _front(x, mod_l, row_of_block, g1, win, sgn, sgw, sgb, qng, kng, rope_tabs, emit_kv):
    n, d = x.shape
    t = FRONT_ROWS
    nb = n // t
    rope = rope_tabs is not None
    const2 = lambda i: (0, 0)
    in_specs = [
        pl.BlockSpec((t, d), lambda i: (i, 0)),
        pl.BlockSpec((1, ADA_CHUNKS, d), lambda i: (row_of_block(i), 0, 0)),
        pl.BlockSpec((1, d), const2),
        pl.BlockSpec(win.shape, const2),
        pl.BlockSpec(sgn.shape, const2),
        pl.BlockSpec(sgw.shape, lambda i: (0, 0, 0)),
        pl.BlockSpec(sgb.shape, const2),
        pl.BlockSpec(qng.shape, const2),
        pl.BlockSpec(kng.shape, const2),
    ]
    args = [x, mod_l, g1, win, sgn, sgw, sgb, qng, kng]
    if rope:
        blocks_per_seq = rope_tabs[0].shape[0] // t
        for tab in rope_tabs:
            in_specs.append(pl.BlockSpec((t, LANES), lambda i: (i % blocks_per_seq, 0)))
            args.append(tab)
    widths = [(256, BF16), (256, BF16), (256, BF16), (256, BF16), (1024, BF16), (128, BF16), (128, BF16)]
    if emit_kv:
        widths += [(256, F32), (256, F32), (128, F32), (128, F32)]
    out_specs = [pl.BlockSpec((t, wd), lambda i: (i, 0)) for wd, _ in widths]
    out_shape = [jax.ShapeDtypeStruct((n, wd), dt) for wd, dt in widths]
    return pl.pallas_call(
        functools.partial(_front_kernel, rope=rope, emit_kv=emit_kv, d_model=d, sgu_w_cols=sgn.shape[1]),
        grid=(nb,),
        in_specs=in_specs,
        out_specs=out_specs,
        out_shape=out_shape,
        compiler_params=_params(1),
        name="front",
    )(*args)


def _softmax_pv(s, v):
    mx = jnp.max(s, axis=-1, keepdims=True)
    e = jnp.exp(s - mx)
    z = jnp.sum(e, axis=-1, keepdims=True)
    return _dot(e.astype(BF16), v) * (1.0 / z)


def _attn_kernel(x_ref, mod_ref, oa_ref, dq_ref, gq_ref, dk_ref, dv_ref, gk_ref, gv_ref,
                 lam_ref, subln_ref, woab_ref, woc_ref, out_ref, oc_ref, *, lam_init):
    lam = (jnp.exp(jnp.sum(lam_ref[0:1] * lam_ref[1:2], axis=-1, keepdims=True))
           - jnp.exp(jnp.sum(lam_ref[2:3] * lam_ref[3:4], axis=-1, keepdims=True)) + lam_init)

    dq = dq_ref[...]
    dk = dk_ref[...]
    dv = dv_ref[...]
    w = dq.shape[1]
    lane = lax.broadcasted_iota(jnp.int32, (1, w), 1)
    out_b = jnp.zeros(dq.shape, F32)
    inv = jnp.zeros(dq.shape, F32)
    n_heads = w // HEAD_DIM
    for hd in range(n_heads):
        parts = []
        for xx in range(2):
            sel = (lane // DIFF_DQK) == (2 * hd + xx)
            qm = jnp.where(sel, dq, jnp.zeros_like(dq))
            parts.append(_softmax_pv(_dot_nt(qm, dk), dv))
        o = parts[0] - lam * parts[1]
        in_head = (lane // HEAD_DIM) == hd
        ms = jnp.sum(jnp.where(in_head, o * o, 0.0), axis=-1, keepdims=True) * (1.0 / HEAD_DIM)
        out_b = jnp.where(in_head, o, out_b)
        inv = jnp.where(in_head, lax.rsqrt(ms + EPS), inv)
    out_b = out_b * inv * subln_ref[...] * (1.0 - lam_init)

    gk = gk_ref[...]
    gv = gv_ref[...]
    for hq in range(gq_ref.shape[1] // LANES):
        cols = slice(hq * LANES, (hq + 1) * LANES)
        oc_ref[:, cols] = _softmax_pv(_dot_nt(gq_ref[:, cols], gk), gv).astype(BF16)

    ab = jnp.concatenate([oa_ref[...], out_b.astype(BF16)], axis=-1)
    mix = _dot(ab, woab_ref[...]) + _dot(oc_ref[...], woc_ref[...])
    out_ref[...] = x_ref[...] + mod_ref[0][2:3] * mix


def _attn(x3, mod_l, row_of_batch, oa, dq, gq, dk, dv, gk, gv, lam_vecs, subln, woab, woc, lam_init):
    b, l, d = x3.shape
    tq = ATTN_ROWS
    lk = dk.shape[1]
    qspec = lambda wd: pl.BlockSpec((None, tq, wd), lambda i, j: (i, j, 0))
    kspec = lambda wd: pl.BlockSpec((None, lk, wd), lambda i, j: (i, 0, 0))
    const2 = lambda i, j: (0, 0)
    return pl.pallas_call(
        functools.partial(_attn_kernel, lam_init=lam_init),
        grid=(b, l // tq),
        in_specs=[
            qspec(d),
            pl.BlockSpec((1, ADA_CHUNKS, d), lambda i, j: (row_of_batch(i), 0, 0)),
            qspec(oa.shape[2]), qspec(dq.shape[2]), qspec(gq.shape[2]),
            kspec(dk.shape[2]), kspec(dv.shape[2]), kspec(gk.shape[2]), kspec(gv.shape[2]),
            pl.BlockSpec(lam_vecs.shape, const2),
            pl.BlockSpec(subln.shape, const2),
            pl.BlockSpec(woab.shape, const2),
            pl.BlockSpec(woc.shape, const2),
        ],
        out_specs=qspec(d),
        out_shape=jax.ShapeDtypeStruct((b, l, d), F32),
        scratch_shapes=[pltpu.VMEM((tq, gq.shape[2]), BF16)],
        compiler_params=_params(2),
        name="attn",
    )(x3, mod_l, oa, dq, gq, dk, dv, gk, gv, lam_vecs, subln, woab, woc)


def _top_rows(s, k, rows_ref):
    cur = s
    rank = jnp.full(s.shape, float(k), F32)
    for r in range(k):
        mx = jnp.max(cur, axis=0, keepdims=True)
        rows_ref[r:r + 1, :] = mx
        hit = cur == mx
        rank = jnp.where(hit, float(r), rank)
        if r + 1 < k:
            cur = jnp.where(hit, -jnp.inf, cur)
    return rank


def _pair_words(x):
    u = pltpu.bitcast(x.astype(BF16).astype(F32), jnp.uint32)
    return u | lax.shift_right_logical(u, jnp.uint32(16))


def _kth_largest(c, k):
    cur = c
    for _ in range(k - 1):
        mx = jnp.max(cur, axis=0, keepdims=True)
        cur = jnp.where(cur == mx, -jnp.inf, cur)
    return jnp.max(cur, axis=0, keepdims=True)


def _route_head(hd, s1, s2, v1_ref, v2_ref, r2_ref, bb_ref, nn_ref, aa_ref, cols):
    _top_rows(s1, PEER_TOPK, v1_ref)
    rank2 = _top_rows(s2, PEER_TOPK, v2_ref)
    v1 = v1_ref[...]
    v2 = v2_ref[...]
    cands = [v1[0:1] + v2]
    cands += [v1[r:r + 1] + v2[0:8] for r in range(1, PEER_TOPK)]
    cand = jnp.concatenate(cands, axis=0)
    tau = _kth_largest(cand, PEER_TOPK)
    top = v1[0:1] + v2[0:1]
    z = jnp.sum(jnp.where(cand >= tau, jnp.exp(cand - top), 0.0), axis=0, keepdims=True)
    n = jnp.zeros(s1.shape, F32)
    for k in range(PEER_TOPK):
        n = n + jnp.where(s1 + v2[k:k + 1] >= tau, 1.0, 0.0)
    a = jnp.exp(s1 - v1[0:1]) * (1.0 / z)
    r2_ref[hd, :, cols] = rank2.astype(BF16)
    bb_ref[hd, :, cols] = jnp.exp(s2 - v2[0:1]).astype(BF16)
    n_words = _pair_words(n)
    a_words = _pair_words(a)
    for k in range(N_KEYS // PEER_KEY_ROWS):
        nn_ref[hd, k, :, cols] = n_words[k * PEER_KEY_ROWS:(k + 1) * PEER_KEY_ROWS]
        aa_ref[hd, k, :, cols] = a_words[k * PEER_KEY_ROWS:(k + 1) * PEER_KEY_ROWS]


def _peer_kernel(x_ref, mod_ref, g2_ref, wqt_ref, sk_ref, u_ref, vt_ref, fin_ref, out_ref,
                 hb_ref, r2_ref, bb_ref, nn_ref, aa_ref, v1_ref, v2_ref, pre_ref, wt_ref, acc_ref,
                 *, final_norm):
    c = pl.program_id(1)
    t = x_ref.shape[0]

    @pl.when(c == 0)
    def _routing():
        m = mod_ref[0]
        h = _rms(x_ref[...]) * g2_ref[...] * (1.0 + m[4:5]) + m[3:4]
        hb = h.astype(BF16)
        hb_ref[...] = hb
        qt = _dot_nt(wqt_ref[...], hb)
        for hd in range(PEER_HEADS):
            base = hd * 2 * PEER_HALF
            s1 = _dot(sk_ref[0], qt[base:base + PEER_HALF].astype(BF16))
            s2 = _dot(sk_ref[1], qt[base + PEER_HALF:base + 2 * PEER_HALF].astype(BF16))
            for lt in range(t // LANES):
                cols = slice(lt * LANES, (lt + 1) * LANES)
                _route_head(hd, s1[:, cols], s2[:, cols], v1_ref, v2_ref, r2_ref, bb_ref, nn_ref, aa_ref, cols)
        acc_ref[...] = jnp.zeros(acc_ref.shape, F32)

    pre_ref[...] = _dot_nt(u_ref[...], hb_ref[...])

    sub = BF16_ROWS
    for r in range(PEER_KEY_ROWS):
        for lt in range(t // LANES):
            cols = slice(lt * LANES, (lt + 1) * LANES)
            n_b, a_b = [], []
            for hd in range(PEER_HEADS):
                n_row = jnp.broadcast_to(nn_ref[hd, c, r:r + 1, cols], (sub // 2, LANES))
                a_row = jnp.broadcast_to(aa_ref[hd, c, r:r + 1, cols], (sub // 2, LANES))
                n_b.append(pltpu.bitcast(n_row, BF16))
                a_b.append(pltpu.bitcast(a_row, BF16))
            for jb in range(N_KEYS // sub):
                keys = slice(jb * sub, (jb + 1) * sub)
                g = jnp.zeros((sub, LANES), BF16)
                for hd in range(PEER_HEADS):
                    kept = jnp.where(r2_ref[hd, keys, cols] < n_b[hd], bb_ref[hd, keys, cols],
                                     jnp.zeros((sub, LANES), BF16))
                    g = g + kept * a_b[hd]
                rows = slice(r * N_KEYS + jb * sub, r * N_KEYS + (jb + 1) * sub)
                wt_ref[rows, cols] = _gelu(pre_ref[rows, cols]).astype(BF16) * g
    acc_ref[...] += _dot(vt_ref[...], wt_ref[...])

    @pl.when(c == pl.num_programs(1) - 1)
    def _finish():
        y = x_ref[...] + mod_ref[0][5:6] * acc_ref[...].T
        if final_norm:
            y = _rms(y) * fin_ref[...]
        out_ref[...] = y


def _peer(x, mod_l, row_of_block, g2, wqt, sk, u, vt, fin, final_norm):
    n, d = x.shape
    t = PEER_ROWS
    ec = PEER_KEY_ROWS * N_KEYS
    nc = u.shape[0] // ec
    const2 = lambda i, c: (0, 0)
    gate = pltpu.VMEM((PEER_HEADS, N_KEYS, t), BF16)
    gate_rows = pltpu.VMEM((PEER_HEADS, N_KEYS // PEER_KEY_ROWS, PEER_KEY_ROWS, t), jnp.uint32)
    return pl.pallas_call(
        functools.partial(_peer_kernel, final_norm=final_norm),
        grid=(n // t, nc),
        in_specs=[
            pl.BlockSpec((t, d), lambda i, c: (i, 0)),
            pl.BlockSpec((1, ADA_CHUNKS, d), lambda i, c: (row_of_block(i), 0, 0)),
            pl.BlockSpec((1, d), const2),
            pl.BlockSpec(wqt.shape, const2),
            pl.BlockSpec(sk.shape, lambda i, c: (0, 0, 0)),
            pl.BlockSpec((ec, d), lambda i, c: (c, 0)),
            pl.BlockSpec((d, ec), lambda i, c: (0, c)),
            pl.BlockSpec((1, d), const2),
        ],
        out_specs=pl.BlockSpec((t, d), lambda i, c: (i, 0)),
        out_shape=jax.ShapeDtypeStruct((n, d), F32),
        scratch_shapes=[
            pltpu.VMEM((t, d), BF16),
            gate, gate, gate_rows, gate_rows,
            pltpu.VMEM((PEER_TOPK, LANES), F32),
            pltpu.VMEM((PEER_TOPK, LANES), F32),
            pltpu.VMEM((ec, t), F32),
            pltpu.VMEM((ec, t), BF16),
            pltpu.VMEM((d, t), F32),
        ],
        compiler_params=_params(2),
        name="peer",
    )(x, mod_l, g2, wqt, sk, u, vt, fin)


def _rope_tables(n_tokens, dim):
    rows = n_tokens // GRID_W
    row = jnp.repeat(jnp.arange(rows, dtype=F32), GRID_W)
    col = jnp.tile(jnp.arange(GRID_W, dtype=F32), rows)
    quarter = dim // 4
    inv = ROPE_THETA ** (-jnp.arange(quarter, dtype=F32) / quarter)
    ang = jnp.concatenate([row[:, None] * inv, col[:, None] * inv], axis=-1)
    cos, sin = jnp.cos(ang), jnp.sin(ang)
    reps = LANES // dim
    return (jnp.tile(jnp.concatenate([cos, cos], axis=-1), (1, reps)),
            jnp.tile(jnp.concatenate([-sin, sin], axis=-1), (1, reps)))


def _slot_pad(a, axis, n_heads):
    parts = []
    for hq in range(n_heads):
        blk = lax.slice_in_dim(a, hq * HEAD_DIM, (hq + 1) * HEAD_DIM, axis=axis)
        zero = jnp.zeros_like(blk)
        parts += [blk, zero] if (hq // GQA_GROUP) == 0 else [zero, blk]
    return jnp.concatenate(parts, axis=axis)


def kernel(x_prompt, x_sample, cache_diff_k, cache_diff_v, cache_gqa_k, cache_gqa_v, c, c_ctx, ada_w, ada_b, norm1_g, norm2_g, w_in, sgu_norm_g, sgu_w, sgu_b, diff_lq1, diff_lk1, diff_lq2, diff_lk2, diff_subln_g, gqa_qnorm_g, gqa_knorm_g, w_out, peer_wq, peer_subkeys, peer_u, peer_v, final_g):
    depth, d, _ = w_in.shape
    bp, lp, _ = x_prompt.shape
    bs, ls, _ = x_sample.shape
    past = cache_diff_k.shape[2]
    sgu_cols = sgu_norm_g.shape[1]
    diff_cols = diff_subln_g.shape[1] * cache_diff_v.shape[3]
    kv_cols = cache_gqa_k.shape[3] * HEAD_DIM
    gq_cols = kv_cols * GQA_GROUP
    n_gqa_heads = gq_cols // HEAD_DIM
    assert kv_cols == LANES and diff_cols == 2 * LANES and sgu_cols == 2 * LANES

    n_rows = -(-(1 + bs) // 8) * 8
    cvecs = jnp.zeros((n_rows, d), F32).at[0].set(c_ctx).at[1:1 + bs].set(c)
    mod = _modulation(cvecs, ada_w, ada_b).reshape(depth, n_rows, ADA_CHUNKS, d)

    o = 2 * sgu_cols + 3 * diff_cols
    win_ext = jnp.concatenate(
        [w_in[:, :, :o], _slot_pad(w_in[:, :, o:o + gq_cols], 2, n_gqa_heads), w_in[:, :, o + gq_cols:]],
        axis=2).astype(BF16)
    woab = w_out[:, :sgu_cols + diff_cols].astype(BF16)
    woc = _slot_pad(w_out[:, sgu_cols + diff_cols:], 1, n_gqa_heads).astype(BF16)
    sgw = sgu_w.astype(BF16)
    sgb = jnp.repeat(jnp.swapaxes(sgu_b, 1, 2), HEAD_DIM, axis=2)
    qng = jnp.tile(gqa_qnorm_g, (1, 2))[:, None, :]
    kng = jnp.tile(gqa_knorm_g, (1, 2))[:, None, :]
    subln = jnp.tile(diff_subln_g, (1, diff_cols // HEAD_DIM))[:, None, :]
    lam_vecs = jnp.stack([diff_lq1, diff_lk1, diff_lq2, diff_lk2], axis=1)
    wqt = jnp.swapaxes(peer_wq, 1, 2).astype(BF16)
    sk = peer_subkeys.astype(BF16)
    u_tab = peer_u.astype(BF16)
    vt_tab = jnp.swapaxes(peer_v, 1, 2).astype(BF16)
    tabs = _rope_tables(ls, DIFF_DQK) + _rope_tables(ls, HEAD_DIM)

    def run(x3, sample):
        b, l, _ = x3.shape
        x = x3.reshape(b * l, d)
        if sample:
            row_front = lambda i: 1 + i // (l // FRONT_ROWS)
            row_attn = lambda i: 1 + i
            row_peer = lambda i: 1 + i // (l // PEER_ROWS)
        else:
            row_front = row_attn = row_peer = lambda i: 0
        kv_out = []
        for layer in range(depth):
            lam_init = 0.8 - 0.6 * math.exp(-0.3 * layer)
            mod_l = mod[layer]
            outs = _front(x, mod_l, row_front, norm1_g[layer][None], win_ext[layer], sgu_norm_g[layer][None],
                          sgw[layer], sgb[layer], qng[layer], kng[layer], tabs if sample else None, not sample)
            oa, dq, dk, dv, gq, gk, gv = [a.reshape(b, l, a.shape[1]) for a in outs[:7]]
            if sample:
                cat = lambda new, old: jnp.concatenate(
                    [new, old[:, layer].reshape(b, past, -1).astype(BF16)], axis=1)
                dk, dv = cat(dk, cache_diff_k), cat(dv, cache_diff_v)
                gk, gv = cat(gk, cache_gqa_k), cat(gv, cache_gqa_v)
            else:
                kv_out.append(outs[7:])
            x3n = _attn(x.reshape(b, l, d), mod_l, row_attn, oa, dq, gq, dk, dv, gk, gv,
                        lam_vecs[layer], subln[layer], woab[layer], woc[layer], lam_init)
            x = _peer(x3n.reshape(b * l, d), mod_l, row_peer, norm2_g[layer][None], wqt[layer], sk[layer],
                      u_tab[layer], vt_tab[layer], final_g[None], layer == depth - 1)
        return x.reshape(b, l, d), kv_out

    y_prompt, kv = run(x_prompt, False)
    y_sample, _ = run(x_sample, True)
    stack = lambda idx, shape: jnp.stack([kv[layer][idx].reshape(shape) for layer in range(depth)], axis=1)
    new_diff_k = stack(0, (bp, lp) + cache_diff_k.shape[3:])
    new_diff_v = stack(1, (bp, lp) + cache_diff_v.shape[3:])
    new_gqa_k = stack(2, (bp, lp) + cache_gqa_k.shape[3:])
    new_gqa_v = stack(3, (bp, lp) + cache_gqa_v.shape[3:])
    return (y_prompt, y_sample, new_diff_k, new_diff_v, new_gqa_k, new_gqa_v)
```

```python
import functools
import math

import jax
import jax.numpy as jnp
from jax import lax
from jax.experimental import pallas as pl
from jax.experimental.pallas import tpu as pltpu

F32 = jnp.float32
BF16 = jnp.bfloat16

HEAD_DIM = 64
SGU_CHUNK = 128
DIFF_DQK = 32
GQA_GROUP = 4
ADA_CHUNKS = 6
N_KEYS = 128
PEER_HEADS = 8
PEER_TOPK = 16
PEER_HALF = 64
GRID_W = 64
ROPE_THETA = 10000.0
EPS = 1e-6
DIFF_SCALE = DIFF_DQK ** -0.5
GQA_SCALE = HEAD_DIM ** -0.5

LANES = 128
BF16_ROWS = 16
VMEM_LIMIT_BYTES = 56 * 1024 * 1024

FRONT_ROWS = 256
ATTN_ROWS = 256
PEER_ROWS = 512
PEER_KEY_ROWS = 8


def _dot(a, b):
    return jnp.dot(a, b, preferred_element_type=F32)


def _dot_nt(a, b):
    return lax.dot_general(a, b, (((1,), (1,)), ((), ())), preferred_element_type=F32)


def _rms(x):
    return x * lax.rsqrt(jnp.mean(x * x, axis=-1, keepdims=True) + EPS)


def _gelu(x):
    c = math.sqrt(2.0 / math.pi)
    return 0.5 * x * (1.0 + jnp.tanh(c * (x + 0.044715 * (x * x * x))))


def _params(n_axes, flags=None):
    return pltpu.CompilerParams(
        dimension_semantics=("arbitrary",) * n_axes, vmem_limit_bytes=VMEM_LIMIT_BYTES, flags=flags)


def _mod_kernel(c_ref, w_ref, b_ref, o_ref):
    c = c_ref[...]
    s = c / (1.0 + jnp.exp(-c))
    o_ref[0] = _dot(s.astype(BF16), w_ref[0].astype(BF16)) + b_ref[0]


def _modulation(cvecs, ada_w, ada_b):
    depth, d, n = ada_w.shape
    rows = cvecs.shape[0]
    tn = n // 4
    return pl.pallas_call(
        _mod_kernel,
        grid=(depth, n // tn),
        in_specs=[
            pl.BlockSpec((rows, d), lambda l, j: (0, 0)),
            pl.BlockSpec((1, d, tn), lambda l, j: (l, 0, j)),
            pl.BlockSpec((1, 1, tn), lambda l, j: (l, 0, j)),
        ],
        out_specs=pl.BlockSpec((1, rows, tn), lambda l, j: (l, 0, j)),
        out_shape=jax.ShapeDtypeStruct((depth, rows, n), F32),
        compiler_params=_params(2),
        name="modulation",
    )(cvecs, ada_w, ada_b.reshape(depth, 1, n))


def _rope(x, cos, sin_signed, half):
    lane = lax.broadcasted_iota(jnp.int32, (1, LANES), 1)
    first = (lane % (2 * half)) < half
    partner = jnp.where(first, pltpu.roll(x, LANES - half, 1), pltpu.roll(x, half, 1))
    return x * cos + partner * sin_signed


def _front_kernel(*refs, rope, emit_kv, d_model, sgu_w_cols):
    (x_ref, mod_ref, g1_ref, win_ref, sgn_ref, sgw_ref, sgb_ref, qng_ref, kng_ref) = refs[:9]
    pos = 9
    if rope:
        cd_ref, sd_ref, cg_ref, sg_ref = refs[pos:pos + 4]
        pos += 4
    oa_ref, dq_ref, dk_ref, dv_ref, gq_ref, gk_ref, gv_ref = refs[pos:pos + 7]
    pos += 7
    if emit_kv:
        kd32_ref, vd32_ref, kg32_ref, vg32_ref = refs[pos:pos + 4]

    x = x_ref[...]
    m = mod_ref[0]
    h = _rms(x) * g1_ref[...] * (1.0 + m[1:2]) + m[0:1]
    proj = _dot(h.astype(BF16), win_ref[...])

    w = sgu_w_cols
    a_u = _gelu(proj[:, 0:w])
    a_v = _rms(_gelu(proj[:, w:2 * w])) * sgn_ref[...]
    a_vb = a_v.astype(BF16)
    lane_w = lax.broadcasted_iota(jnp.int32, (1, w), 1)
    n_groups = w // HEAD_DIM
    t = x.shape[0]
    for ch in range(t // SGU_CHUNK):
        rows = slice(ch * SGU_CHUNK, (ch + 1) * SGU_CHUNK)
        av = a_vb[rows]
        mix = jnp.zeros((SGU_CHUNK, w), F32)
        for g in range(n_groups):
            part = _dot(sgw_ref[g], av)
            mix = jnp.where(lane_w // HEAD_DIM == g, part, mix)
        oa_ref[rows, :] = (a_u[rows] * (mix + sgb_ref[...])).astype(BF16)

    o = 2 * w
    d_q = proj[:, o:o + 256]
    d_k = proj[:, o + 256:o + 512]
    d_v = proj[:, o + 512:o + 768]
    g_q = proj[:, o + 768:o + 1792]
    g_k = proj[:, o + 1792:o + 1920]
    g_v = proj[:, o + 1920:o + 2048]

    lane = lax.broadcasted_iota(jnp.int32, (1, LANES), 1)
    low = lane < HEAD_DIM
    k2 = g_k * g_k
    s_low = jnp.sum(jnp.where(low, k2, 0.0), axis=-1, keepdims=True)
    s_all = jnp.sum(k2, axis=-1, keepdims=True)
    ms = jnp.where(low, s_low, s_all - s_low) * (1.0 / HEAD_DIM)
    g_k = g_k * lax.rsqrt(ms + EPS) * kng_ref[...]

    if emit_kv:
        kd32_ref[...] = d_k
        vd32_ref[...] = d_v
        kg32_ref[...] = g_k
        vg32_ref[...] = g_v

    if rope:
        g_k = _rope(g_k, cg_ref[...], sg_ref[...], HEAD_DIM // 2)
    gk_ref[...] = g_k.astype(BF16)
    gv_ref[...] = g_v.astype(BF16)
    dv_ref[...] = d_v.astype(BF16)

    for s in range(2):
        cols = slice(s * LANES, (s + 1) * LANES)
        q = d_q[:, cols]
        k = d_k[:, cols]
        if rope:
            q = _rope(q, cd_ref[...], sd_ref[...], DIFF_DQK // 2)
            k = _rope(k, cd_ref[...], sd_ref[...], DIFF_DQK // 2)
        dq_ref[:, cols] = (q * DIFF_SCALE).astype(BF16)
        dk_ref[:, cols] = k.astype(BF16)

    for s in range(g_q.shape[1] // LANES):
        cols = slice(s * LANES, (s + 1) * LANES)
        q = g_q[:, cols]
        ms = jnp.sum(q * q, axis=-1, keepdims=True) * (1.0 / HEAD_DIM)
        q = q * lax.rsqrt(ms + EPS) * qng_ref[...]
        if rope:
            q = _rope(q, cg_ref[...], sg_ref[...], HEAD_DIM // 2)
        gq_ref[:, cols] = (q * GQA_SCALE).astype(BF16)


def _front(x, mod_l, row_of_block, g1, win, sgn, sgw, sgb, qng, kng, rope_tabs, emit_kv):
    n, d = x.shape
    t = FRONT_ROWS
    nb = n // t
    rope = rope_tabs is not None
    const2 = lambda i: (0, 0)
    in_specs = [
        pl.BlockSpec((t, d), lambda i: (i, 0)),
        pl.BlockSpec((1, ADA_CHUNKS, d), lambda i: (row_of_block(i), 0, 0)),
        pl.BlockSpec((1, d), const2),
        pl.BlockSpec(win.shape, const2),
        pl.BlockSpec(sgn.shape, const2),
        pl.BlockSpec(sgw.shape, lambda i: (0, 0, 0)),
        pl.BlockSpec(sgb.shape, const2),
        pl.BlockSpec(qng.shape, const2),
        pl.BlockSpec(kng.shape, const2),
    ]
    args = [x, mod_l, g1, win, sgn, sgw, sgb, qng, kng]
    if rope:
        blocks_per_seq = rope_tabs[0].shape[0] // t
        for tab in rope_tabs:
            in_specs.append(pl.BlockSpec((t, LANES), lambda i: (i % blocks_per_seq, 0)))
            args.append(tab)
    widths = [(256, BF16), (256, BF16), (256, BF16), (256, BF16), (1024, BF16), (128, BF16), (128, BF16)]
    if emit_kv:
        widths += [(256, F32), (256, F32), (128, F32), (128, F32)]
    out_specs = [pl.BlockSpec((t, wd), lambda i: (i, 0)) for wd, _ in widths]
    out_shape = [jax.ShapeDtypeStruct((n, wd), dt) for wd, dt in widths]
    return pl.pallas_call(
        functools.partial(_front_kernel, rope=rope, emit_kv=emit_kv, d_model=d, sgu_w_cols=sgn.shape[1]),
        grid=(nb,),
        in_specs=in_specs,
        out_specs=out_specs,
        out_shape=out_shape,
        compiler_params=_params(1),
        name="front",
    )(*args)


def _softmax_pv(s, v):
    mx = jnp.max(s, axis=-1, keepdims=True)
    e = jnp.exp(s - mx)
    z = jnp.sum(e, axis=-1, keepdims=True)
    return _dot(e.astype(BF16), v) * (1.0 / z)


def _attn_kernel(x_ref, mod_ref, oa_ref, dq_ref, gq_ref, dk_ref, dv_ref, gk_ref, gv_ref,
                 lam_ref, subln_ref, woab_ref, woc_ref, out_ref, oc_ref, *, lam_init):
    lam = (jnp.exp(jnp.sum(lam_ref[0:1] * lam_ref[1:2], axis=-1, keepdims=True))
           - jnp.exp(jnp.sum(lam_ref[2:3] * lam_ref[3:4], axis=-1, keepdims=True)) + lam_init)

    dq = dq_ref[...]
    dk = dk_ref[...]
    dv = dv_ref[...]
    w = dq.shape[1]
    lane = lax.broadcasted_iota(jnp.int32, (1, w), 1)
    out_b = jnp.zeros(dq.shape, F32)
    inv = jnp.zeros(dq.shape, F32)
    n_heads = w // HEAD_DIM
    for hd in range(n_heads):
        parts = []
        for xx in range(2):
            sel = (lane // DIFF_DQK) == (2 * hd + xx)
            qm = jnp.where(sel, dq, jnp.zeros_like(dq))
            parts.append(_softmax_pv(_dot_nt(qm, dk), dv))
        o = parts[0] - lam * parts[1]
        in_head = (lane // HEAD_DIM) == hd
        ms = jnp.sum(jnp.where(in_head, o * o, 0.0), axis=-1, keepdims=True) * (1.0 / HEAD_DIM)
        out_b = jnp.where(in_head, o, out_b)
        inv = jnp.where(in_head, lax.rsqrt(ms + EPS), inv)
    out_b = out_b * inv * subln_ref[...] * (1.0 - lam_init)

    gk = gk_ref[...]
    gv = gv_ref[...]
    for hq in range(gq_ref.shape[1] // LANES):
        cols = slice(hq * LANES, (hq + 1) * LANES)
        oc_ref[:, cols] = _softmax_pv(_dot_nt(gq_ref[:, cols], gk), gv).astype(BF16)

    ab = jnp.concatenate([oa_ref[...], out_b.astype(BF16)], axis=-1)
    mix = _dot(ab, woab_ref[...]) + _dot(oc_ref[...], woc_ref[...])
    out_ref[...] = x_ref[...] + mod_ref[0][2:3] * mix


def _attn(x3, mod_l, row_of_batch, oa, dq, gq, dk, dv, gk, gv, lam_vecs, subln, woab, woc, lam_init):
    b, l, d = x3.shape
    tq = ATTN_ROWS
    lk = dk.shape[1]
    qspec = lambda wd: pl.BlockSpec((None, tq, wd), lambda i, j: (i, j, 0))
    kspec = lambda wd: pl.BlockSpec((None, lk, wd), lambda i, j: (i, 0, 0))
    const2 = lambda i, j: (0, 0)
    return pl.pallas_call(
        functools.partial(_attn_kernel, lam_init=lam_init),
        grid=(b, l // tq),
        in_specs=[
            qspec(d),
            pl.BlockSpec((1, ADA_CHUNKS, d), lambda i, j: (row_of_batch(i), 0, 0)),
            qspec(oa.shape[2]), qspec(dq.shape[2]), qspec(gq.shape[2]),
            kspec(dk.shape[2]), kspec(dv.shape[2]), kspec(gk.shape[2]), kspec(gv.shape[2]),
            pl.BlockSpec(lam_vecs.shape, const2),
            pl.BlockSpec(subln.shape, const2),
            pl.BlockSpec(woab.shape, const2),
            pl.BlockSpec(woc.shape, const2),
        ],
        out_specs=qspec(d),
        out_shape=jax.ShapeDtypeStruct((b, l, d), F32),
        scratch_shapes=[pltpu.VMEM((tq, gq.shape[2]), BF16)],
        compiler_params=_params(2),
        name="attn",
    )(x3, mod_l, oa, dq, gq, dk, dv, gk, gv, lam_vecs, subln, woab, woc)


def _top_rows(s, k, rows_ref):
    cur = s
    rank = jnp.full(s.shape, float(k), F32)
    for r in range(k):
        mx = jnp.max(cur, axis=0, keepdims=True)
        rows_ref[r:r + 1, :] = mx
        hit = cur == mx
        rank = jnp.where(hit, float(r), rank)
        if r + 1 < k:
            cur = jnp.where(hit, -jnp.inf, cur)
    return rank


def _pair_words(x):
    u = pltpu.bitcast(x.astype(BF16).astype(F32), jnp.uint32)
    return u | lax.shift_right_logical(u, jnp.uint32(16))


def _kth_largest(c, k):
    cur = c
    for _ in range(k - 1):
        mx = jnp.max(cur, axis=0, keepdims=True)
        cur = jnp.where(cur == mx, -jnp.inf, cur)
    return jnp.max(cur, axis=0, keepdims=True)


def _route_head(hd, s1, s2, v1_ref, v2_ref, r2_ref, bb_ref, nn_ref, aa_ref, cols):
    _top_rows(s1, PEER_TOPK, v1_ref)
    rank2 = _top_rows(s2, PEER_TOPK, v2_ref)
    v1 = v1_ref[...]
    v2 = v2_ref[...]
    cands = [v1[0:1] + v2]
    cands += [v1[r:r + 1] + v2[0:8] for r in range(1, PEER_TOPK)]
    cand = jnp.concatenate(cands, axis=0)
    tau = _kth_largest(cand, PEER_TOPK)
    top = v1[0:1] + v2[0:1]
    z = jnp.sum(jnp.where(cand >= tau, jnp.exp(cand - top), 0.0), axis=0, keepdims=True)
    n = jnp.zeros(s1.shape, F32)
    for k in range(PEER_TOPK):
        n = n + jnp.where(s1 + v2[k:k + 1] >= tau, 1.0, 0.0)
    a = jnp.exp(s1 - v1[0:1]) * (1.0 / z)
    r2_ref[hd, :, cols] = rank2.astype(BF16)
    bb_ref[hd, :, cols] = jnp.exp(s2 - v2[0:1]).astype(BF16)
    n_words = _pair_words(n)
    a_words = _pair_words(a)
    for k in range(N_KEYS // PEER_KEY_ROWS):
        nn_ref[hd, k, :, cols] = n_words[k * PEER_KEY_ROWS:(k + 1) * PEER_KEY_ROWS]
        aa_ref[hd, k, :, cols] = a_words[k * PEER_KEY_ROWS:(k + 1) * PEER_KEY_ROWS]


def _peer_kernel(x_ref, mod_ref, g2_ref, wqt_ref, sk_ref, u_ref, vt_ref, fin_ref, out_ref,
                 hb_ref, r2_ref, bb_ref, nn_ref, aa_ref, v1_ref, v2_ref, pre_ref, wt_ref, acc_ref,
                 *, final_norm):
    c = pl.program_id(1)
    t = x_ref.shape[0]

    @pl.when(c == 0)
    def _routing():
        m = mod_ref[0]
        h = _rms(x_ref[...]) * g2_ref[...] * (1.0 + m[4:5]) + m[3:4]
        hb = h.astype(BF16)
        hb_ref[...] = hb
        qt = _dot_nt(wqt_ref[...], hb)
        for hd in range(PEER_HEADS):
            base = hd * 2 * PEER_HALF
            s1 = _dot(sk_ref[0], qt[base:base + PEER_HALF].astype(BF16))
            s2 = _dot(sk_ref[1], qt[base + PEER_HALF:base + 2 * PEER_HALF].astype(BF16))
            for lt in range(t // LANES):
                cols = slice(lt * LANES, (lt + 1) * LANES)
                _route_head(hd, s1[:, cols], s2[:, cols], v1_ref, v2_ref, r2_ref, bb_ref, nn_ref, aa_ref, cols)
        acc_ref[...] = jnp.zeros(acc_ref.shape, F32)

    pre_ref[...] = _dot_nt(u_ref[...], hb_ref[...])

    sub = BF16_ROWS
    for r in range(PEER_KEY_ROWS):
        for lt in range(t // LANES):
            cols = slice(lt * LANES, (lt + 1) * LANES)
            n_b, a_b = [], []
            for hd in range(PEER_HEADS):
                n_row = jnp.broadcast_to(nn_ref[hd, c, r:r + 1, cols], (sub // 2, LANES))
                a_row = jnp.broadcast_to(aa_ref[hd, c, r:r + 1, cols], (sub // 2, LANES))
                n_b.append(pltpu.bitcast(n_row, BF16))
                a_b.append(pltpu.bitcast(a_row, BF16))
            for jb in range(N_KEYS // sub):
                keys = slice(jb * sub, (jb + 1) * sub)
                g = jnp.zeros((sub, LANES), BF16)
                for hd in range(PEER_HEADS):
                    kept = jnp.where(r2_ref[hd, keys, cols] < n_b[hd], bb_ref[hd, keys, cols],
                                     jnp.zeros((sub, LANES), BF16))
                    g = g + kept * a_b[hd]
                rows = slice(r * N_KEYS + jb * sub, r * N_KEYS + (jb + 1) * sub)
                wt_ref[rows, cols] = _gelu(pre_ref[rows, cols]).astype(BF16) * g
    acc_ref[...] += _dot(vt_ref[...], wt_ref[...])

    @pl.when(c == pl.num_programs(1) - 1)
    def _finish():
        y = x_ref[...] + mod_ref[0][5:6] * acc_ref[...].T
        if final_norm:
            y = _rms(y) * fin_ref[...]
        out_ref[...] = y


def _peer(x, mod_l, row_of_block, g2, wqt, sk, u, vt, fin, final_norm):
    n, d = x.shape
    t = PEER_ROWS
    ec = PEER_KEY_ROWS * N_KEYS
    nc = u.shape[0] // ec
    const2 = lambda i, c: (0, 0)
    gate = pltpu.VMEM((PEER_HEADS, N_KEYS, t), BF16)
    gate_rows = pltpu.VMEM((PEER_HEADS, N_KEYS // PEER_KEY_ROWS, PEER_KEY_ROWS, t), jnp.uint32)
    return pl.pallas_call(
        functools.partial(_peer_kernel, final_norm=final_norm),
        grid=(n // t, nc),
        in_specs=[
            pl.BlockSpec((t, d), lambda i, c: (i, 0)),
            pl.BlockSpec((1, ADA_CHUNKS, d), lambda i, c: (row_of_block(i), 0, 0)),
            pl.BlockSpec((1, d), const2),
            pl.BlockSpec(wqt.shape, const2),
            pl.BlockSpec(sk.shape, lambda i, c: (0, 0, 0)),
            pl.BlockSpec((ec, d), lambda i, c: (c, 0)),
            pl.BlockSpec((None, d, ec), lambda i, c: (c, 0, 0)),
            pl.BlockSpec((1, d), const2),
        ],
        out_specs=pl.BlockSpec((t, d), lambda i, c: (i, 0)),
        out_shape=jax.ShapeDtypeStruct((n, d), F32),
        scratch_shapes=[
            pltpu.VMEM((t, d), BF16),
            gate, gate, gate_rows, gate_rows,
            pltpu.VMEM((PEER_TOPK, LANES), F32),
            pltpu.VMEM((PEER_TOPK, LANES), F32),
            pltpu.VMEM((ec, t), F32),
            pltpu.VMEM((ec, t), BF16),
            pltpu.VMEM((d, t), F32),
        ],
        compiler_params=_params(2),
        name="peer",
    )(x, mod_l, g2, wqt, sk, u, vt, fin)


def _rope_tables(n_tokens, dim):
    rows = n_tokens // GRID_W
    row = jnp.repeat(jnp.arange(rows, dtype=F32), GRID_W)
    col = jnp.tile(jnp.arange(GRID_W, dtype=F32), rows)
    quarter = dim // 4
    inv = ROPE_THETA ** (-jnp.arange(quarter, dtype=F32) / quarter)
    ang = jnp.concatenate([row[:, None] * inv, col[:, None] * inv], axis=-1)
    cos, sin = jnp.cos(ang), jnp.sin(ang)
    reps = LANES // dim
    return (jnp.tile(jnp.concatenate([cos, cos], axis=-1), (1, reps)),
            jnp.tile(jnp.concatenate([-sin, sin], axis=-1), (1, reps)))


def _slot_pad(a, axis, n_heads):
    parts = []
    for hq in range(n_heads):
        blk = lax.slice_in_dim(a, hq * HEAD_DIM, (hq + 1) * HEAD_DIM, axis=axis)
        zero = jnp.zeros_like(blk)
        parts += [blk, zero] if (hq // GQA_GROUP) == 0 else [zero, blk]
    return jnp.concatenate(parts, axis=axis)


def kernel(x_prompt, x_sample, cache_diff_k, cache_diff_v, cache_gqa_k, cache_gqa_v, c, c_ctx, ada_w, ada_b, norm1_g, norm2_g, w_in, sgu_norm_g, sgu_w, sgu_b, diff_lq1, diff_lk1, diff_lq2, diff_lk2, diff_subln_g, gqa_qnorm_g, gqa_knorm_g, w_out, peer_wq, peer_subkeys, peer_u, peer_v, final_g):
    depth, d, _ = w_in.shape
    bp, lp, _ = x_prompt.shape
    bs, ls, _ = x_sample.shape
    past = cache_diff_k.shape[2]
    sgu_cols = sgu_norm_g.shape[1]
    diff_cols = diff_subln_g.shape[1] * cache_diff_v.shape[3]
    kv_cols = cache_gqa_k.shape[3] * HEAD_DIM
    gq_cols = kv_cols * GQA_GROUP
    n_gqa_heads = gq_cols // HEAD_DIM
    assert kv_cols == LANES and diff_cols == 2 * LANES and sgu_cols == 2 * LANES

    n_rows = -(-(1 + bs) // 8) * 8
    cvecs = jnp.zeros((n_rows, d), F32).at[0].set(c_ctx).at[1:1 + bs].set(c)
    mod = _modulation(cvecs, ada_w, ada_b).reshape(depth, n_rows, ADA_CHUNKS, d)

    o = 2 * sgu_cols + 3 * diff_cols
    win_ext = jnp.concatenate(
        [w_in[:, :, :o], _slot_pad(w_in[:, :, o:o + gq_cols], 2, n_gqa_heads), w_in[:, :, o + gq_cols:]],
        axis=2).astype(BF16)
    woab = w_out[:, :sgu_cols + diff_cols].astype(BF16)
    woc = _slot_pad(w_out[:, sgu_cols + diff_cols:], 1, n_gqa_heads).astype(BF16)
    sgw = sgu_w.astype(BF16)
    sgb = jnp.repeat(jnp.swapaxes(sgu_b, 1, 2), HEAD_DIM, axis=2)
    qng = jnp.tile(gqa_qnorm_g, (1, 2))[:, None, :]
    kng = jnp.tile(gqa_knorm_g, (1, 2))[:, None, :]
    subln = jnp.tile(diff_subln_g, (1, diff_cols // HEAD_DIM))[:, None, :]
    lam_vecs = jnp.stack([diff_lq1, diff_lk1, diff_lq2, diff_lk2], axis=1)
    wqt = jnp.swapaxes(peer_wq, 1, 2).astype(BF16)
    sk = peer_subkeys.astype(BF16)
    u_tab = peer_u.astype(BF16)
    ec = PEER_KEY_ROWS * N_KEYS
    vt_tab = jnp.swapaxes(peer_v.astype(BF16).reshape(depth, -1, ec, d), 2, 3)
    tabs = _rope_tables(ls, DIFF_DQK) + _rope_tables(ls, HEAD_DIM)

    def run(x3, sample):
        b, l, _ = x3.shape
        x = x3.reshape(b * l, d)
        if sample:
            row_front = lambda i: 1 + i // (l // FRONT_ROWS)
            row_attn = lambda i: 1 + i
            row_peer = lambda i: 1 + i // (l // PEER_ROWS)
        else:
            row_front = row_attn = row_peer = lambda i: 0
        kv_out = []
        for layer in range(depth):
            lam_init = 0.8 - 0.6 * math.exp(-0.3 * layer)
            mod_l = mod[layer]
            outs = _front(x, mod_l, row_front, norm1_g[layer][None], win_ext[layer], sgu_norm_g[layer][None],
                          sgw[layer], sgb[layer], qng[layer], kng[layer], tabs if sample else None, not sample)
            oa, dq, dk, dv, gq, gk, gv = [a.reshape(b, l, a.shape[1]) for a in outs[:7]]
            if sample:
                cat = lambda new, old: jnp.concatenate(
                    [new, old[:, layer].reshape(b, past, -1).astype(BF16)], axis=1)
                dk, dv = cat(dk, cache_diff_k), cat(dv, cache_diff_v)
                gk, gv = cat(gk, cache_gqa_k), cat(gv, cache_gqa_v)
            else:
                kv_out.append(outs[7:])
            x3n = _attn(x.reshape(b, l, d), mod_l, row_attn, oa, dq, gq, dk, dv, gk, gv,
                        lam_vecs[layer], subln[layer], woab[layer], woc[layer], lam_init)
            x = _peer(x3n.reshape(b * l, d), mod_l, row_peer, norm2_g[layer][None], wqt[layer], sk[layer],
                      u_tab[layer], vt_tab[layer], final_g[None], layer == depth - 1)
        return x.reshape(b, l, d), kv_out

    y_prompt, kv = run(x_prompt, False)
    y_sample, _ = run(x_sample, True)
    stack = lambda idx, shape: jnp.stack([kv[layer][idx].reshape(shape) for layer in range(depth)], axis=1)
    new_diff_k = stack(0, (bp, lp) + cache_diff_k.shape[3:])
    new_diff_v = stack(1, (bp, lp) + cache_diff_v.shape[3:])
    new_gqa_k = stack(2, (bp, lp) + cache_gqa_k.shape[3:])
    new_gqa_v = stack(3, (bp, lp) + cache_gqa_v.shape[3:])
    return (y_prompt, y_sample, new_diff_k, new_diff_v, new_gqa_k, new_gqa_v)
```

```python
import functools
import math

import jax
import jax.numpy as jnp
from jax import lax
from jax.experimental import pallas as pl
from jax.experimental.pallas import tpu as pltpu

F32 = jnp.float32
BF16 = jnp.bfloat16

HEAD_DIM = 64
SGU_CHUNK = 128
DIFF_DQK = 32
GQA_GROUP = 4
ADA_CHUNKS = 6
N_KEYS = 128
PEER_HEADS = 8
PEER_TOPK = 16
PEER_HALF = 64
GRID_W = 64
ROPE_THETA = 10000.0
EPS = 1e-6
DIFF_SCALE = DIFF_DQK ** -0.5
GQA_SCALE = HEAD_DIM ** -0.5

LANES = 128
BF16_ROWS = 16
VMEM_LIMIT_BYTES = 56 * 1024 * 1024

FRONT_ROWS = 256
ATTN_ROWS = 256
PEER_ROWS = 512
PEER_KEY_ROWS = 8


def _dot(a, b):
    return jnp.dot(a, b, preferred_element_type=F32)


def _dot_nt(a, b):
    return lax.dot_general(a, b, (((1,), (1,)), ((), ())), preferred_element_type=F32)


def _rms(x):
    return x * lax.rsqrt(jnp.mean(x * x, axis=-1, keepdims=True) + EPS)


def _gelu(x):
    c = math.sqrt(2.0 / math.pi)
    return 0.5 * x * (1.0 + jnp.tanh(c * (x + 0.044715 * (x * x * x))))


def _gelu_folded(x):
    c = math.sqrt(2.0 / math.pi)
    hx = 0.5 * x
    return hx + hx * jnp.tanh(x * (c + (0.044715 * c) * (x * x)))


def _params(n_axes, flags=None):
    return pltpu.CompilerParams(
        dimension_semantics=("arbitrary",) * n_axes, vmem_limit_bytes=VMEM_LIMIT_BYTES, flags=flags)


def _mod_kernel(c_ref, w_ref, b_ref, o_ref):
    c = c_ref[...]
    s = c / (1.0 + jnp.exp(-c))
    o_ref[0] = _dot(s.astype(BF16), w_ref[0].astype(BF16)) + b_ref[0]


def _modulation(cvecs, ada_w, ada_b):
    depth, d, n = ada_w.shape
    rows = cvecs.shape[0]
    tn = n // 4
    return pl.pallas_call(
        _mod_kernel,
        grid=(depth, n // tn),
        in_specs=[
            pl.BlockSpec((rows, d), lambda l, j: (0, 0)),
            pl.BlockSpec((1, d, tn), lambda l, j: (l, 0, j)),
            pl.BlockSpec((1, 1, tn), lambda l, j: (l, 0, j)),
        ],
        out_specs=pl.BlockSpec((1, rows, tn), lambda l, j: (l, 0, j)),
        out_shape=jax.ShapeDtypeStruct((depth, rows, n), F32),
        compiler_params=_params(2),
        name="modulation",
    )(cvecs, ada_w, ada_b.reshape(depth, 1, n))


def _rope(x, cos, sin_signed, half):
    lane = lax.broadcasted_iota(jnp.int32, (1, LANES), 1)
    first = (lane % (2 * half)) < half
    partner = jnp.where(first, pltpu.roll(x, LANES - half, 1), pltpu.roll(x, half, 1))
    return x * cos + partner * sin_signed


def _front_kernel(*refs, rope, emit_kv, d_model, sgu_w_cols):
    (x_ref, mod_ref, g1_ref, win_ref, sgn_ref, sgw_ref, sgb_ref, qng_ref, kng_ref) = refs[:9]
    pos = 9
    if rope:
        cd_ref, sd_ref, cg_ref, sg_ref = refs[pos:pos + 4]
        pos += 4
    oa_ref, dq_ref, dk_ref, dv_ref, gq_ref, gk_ref, gv_ref = refs[pos:pos + 7]
    pos += 7
    if emit_kv:
        kd32_ref, vd32_ref, kg32_ref, vg32_ref = refs[pos:pos + 4]

    x = x_ref[...]
    m = mod_ref[0]
    h = _rms(x) * g1_ref[...] * (1.0 + m[1:2]) + m[0:1]
    proj = _dot(h.astype(BF16), win_ref[...])

    w = sgu_w_cols
    a_u = _gelu(proj[:, 0:w])
    a_v = _rms(_gelu(proj[:, w:2 * w])) * sgn_ref[...]
    a_vb = a_v.astype(BF16)
    lane_w = lax.broadcasted_iota(jnp.int32, (1, w), 1)
    n_groups = w // HEAD_DIM
    t = x.shape[0]
    for ch in range(t // SGU_CHUNK):
        rows = slice(ch * SGU_CHUNK, (ch + 1) * SGU_CHUNK)
        av = a_vb[rows]
        mix = jnp.zeros((SGU_CHUNK, w), F32)
        for g in range(n_groups):
            part = _dot(sgw_ref[g], av)
            mix = jnp.where(lane_w // HEAD_DIM == g, part, mix)
        oa_ref[rows, :] = (a_u[rows] * (mix + sgb_ref[...])).astype(BF16)

    o = 2 * w
    d_q = proj[:, o:o + 256]
    d_k = proj[:, o + 256:o + 512]
    d_v = proj[:, o + 512:o + 768]
    g_q = proj[:, o + 768:o + 1792]
    g_k = proj[:, o + 1792:o + 1920]
    g_v = proj[:, o + 1920:o + 2048]

    lane = lax.broadcasted_iota(jnp.int32, (1, LANES), 1)
    low = lane < HEAD_DIM
    k2 = g_k * g_k
    s_low = jnp.sum(jnp.where(low, k2, 0.0), axis=-1, keepdims=True)
    s_all = jnp.sum(k2, axis=-1, keepdims=True)
    ms = jnp.where(low, s_low, s_all - s_low) * (1.0 / HEAD_DIM)
    g_k = g_k * lax.rsqrt(ms + EPS) * kng_ref[...]

    if emit_kv:
        kd32_ref[...] = d_k
        vd32_ref[...] = d_v
        kg32_ref[...] = g_k
        vg32_ref[...] = g_v

    if rope:
        g_k = _rope(g_k, cg_ref[...], sg_ref[...], HEAD_DIM // 2)
    gk_ref[...] = g_k.astype(BF16)
    gv_ref[...] = g_v.astype(BF16)
    dv_ref[...] = d_v.astype(BF16)

    for s in range(2):
        cols = slice(s * LANES, (s + 1) * LANES)
        q = d_q[:, cols]
        k = d_k[:, cols]
        if rope:
            q = _rope(q, cd_ref[...], sd_ref[...], DIFF_DQK // 2)
            k = _rope(k, cd_ref[...], sd_ref[...], DIFF_DQK // 2)
        dq_ref[:, cols] = (q * DIFF_SCALE).astype(BF16)
        dk_ref[:, cols] = k.astype(BF16)

    for s in range(g_q.shape[1] // LANES):
        cols = slice(s * LANES, (s + 1) * LANES)
        q = g_q[:, cols]
        ms = jnp.sum(q * q, axis=-1, keepdims=True) * (1.0 / HEAD_DIM)
        q = q * lax.rsqrt(ms + EPS) * qng_ref[...]
        if rope:
            q = _rope(q, cg_ref[...], sg_ref[...], HEAD_DIM // 2)
        gq_ref[:, cols] = (q * GQA_SCALE).astype(BF16)


def _front(x, mod_l, row_of_block, g1, win, sgn, sgw, sgb, qng, kng, rope_tabs, emit_kv):
    n, d = x.shape
    t = FRONT_ROWS
    nb = n // t
    rope = rope_tabs is not None
    const2 = lambda i: (0, 0)
    in_specs = [
        pl.BlockSpec((t, d), lambda i: (i, 0)),
        pl.BlockSpec((1, ADA_CHUNKS, d), lambda i: (row_of_block(i), 0, 0)),
        pl.BlockSpec((1, d), const2),
        pl.BlockSpec(win.shape, const2),
        pl.BlockSpec(sgn.shape, const2),
        pl.BlockSpec(sgw.shape, lambda i: (0, 0, 0)),
        pl.BlockSpec(sgb.shape, const2),
        pl.BlockSpec(qng.shape, const2),
        pl.BlockSpec(kng.shape, const2),
    ]
    args = [x, mod_l, g1, win, sgn, sgw, sgb, qng, kng]
    if rope:
        blocks_per_seq = rope_tabs[0].shape[0] // t
        for tab in rope_tabs:
            in_specs.append(pl.BlockSpec((t, LANES), lambda i: (i % blocks_per_seq, 0)))
            args.append(tab)
    widths = [(256, BF16), (256, BF16), (256, BF16), (256, BF16), (1024, BF16), (128, BF16), (128, BF16)]
    if emit_kv:
        widths += [(256, F32), (256, F32), (128, F32), (128, F32)]
    out_specs = [pl.BlockSpec((t, wd), lambda i: (i, 0)) for wd, _ in widths]
    out_shape = [jax.ShapeDtypeStruct((n, wd), dt) for wd, dt in widths]
    return pl.pallas_call(
        functools.partial(_front_kernel, rope=rope, emit_kv=emit_kv, d_model=d, sgu_w_cols=sgn.shape[1]),
        grid=(nb,),
        in_specs=in_specs,
        out_specs=out_specs,
        out_shape=out_shape,
        compiler_params=_params(1),
        name="front",
    )(*args)


def _softmax_pv(s, v):
    mx = jnp.max(s, axis=-1, keepdims=True)
    e = jnp.exp(s - mx)
    z = jnp.sum(e, axis=-1, keepdims=True)
    return _dot(e.astype(BF16), v) * (1.0 / z)


def _attn_kernel(x_ref, mod_ref, oa_ref, dq_ref, gq_ref, dk_ref, dv_ref, gk_ref, gv_ref,
                 lam_ref, subln_ref, woab_ref, woc_ref, out_ref, oc_ref, *, lam_init):
    lam = (jnp.exp(jnp.sum(lam_ref[0:1] * lam_ref[1:2], axis=-1, keepdims=True))
           - jnp.exp(jnp.sum(lam_ref[2:3] * lam_ref[3:4], axis=-1, keepdims=True)) + lam_init)

    dq = dq_ref[...]
    dk = dk_ref[...]
    dv = dv_ref[...]
    w = dq.shape[1]
    lane = lax.broadcasted_iota(jnp.int32, (1, w), 1)
    out_b = jnp.zeros(dq.shape, F32)
    inv = jnp.zeros(dq.shape, F32)
    n_heads = w // HEAD_DIM
    for hd in range(n_heads):
        parts = []
        for xx in range(2):
            sel = (lane // DIFF_DQK) == (2 * hd + xx)
            qm = jnp.where(sel, dq, jnp.zeros_like(dq))
            parts.append(_softmax_pv(_dot_nt(qm, dk), dv))
        o = parts[0] - lam * parts[1]
        in_head = (lane // HEAD_DIM) == hd
        ms = jnp.sum(jnp.where(in_head, o * o, 0.0), axis=-1, keepdims=True) * (1.0 / HEAD_DIM)
        out_b = jnp.where(in_head, o, out_b)
        inv = jnp.where(in_head, lax.rsqrt(ms + EPS), inv)
    out_b = out_b * inv * subln_ref[...] * (1.0 - lam_init)

    gk = gk_ref[...]
    gv = gv_ref[...]
    for hq in range(gq_ref.shape[1] // LANES):
        cols = slice(hq * LANES, (hq + 1) * LANES)
        oc_ref[:, cols] = _softmax_pv(_dot_nt(gq_ref[:, cols], gk), gv).astype(BF16)

    ab = jnp.concatenate([oa_ref[...], out_b.astype(BF16)], axis=-1)
    mix = _dot(ab, woab_ref[...]) + _dot(oc_ref[...], woc_ref[...])
    out_ref[...] = x_ref[...] + mod_ref[0][2:3] * mix


def _attn(x3, mod_l, row_of_batch, oa, dq, gq, dk, dv, gk, gv, lam_vecs, subln, woab, woc, lam_init):
    b, l, d = x3.shape
    tq = ATTN_ROWS
    lk = dk.shape[1]
    qspec = lambda wd: pl.BlockSpec((None, tq, wd), lambda i, j: (i, j, 0))
    kspec = lambda wd: pl.BlockSpec((None, lk, wd), lambda i, j: (i, 0, 0))
    const2 = lambda i, j: (0, 0)
    return pl.pallas_call(
        functools.partial(_attn_kernel, lam_init=lam_init),
        grid=(b, l // tq),
        in_specs=[
            qspec(d),
            pl.BlockSpec((1, ADA_CHUNKS, d), lambda i, j: (row_of_batch(i), 0, 0)),
            qspec(oa.shape[2]), qspec(dq.shape[2]), qspec(gq.shape[2]),
            kspec(dk.shape[2]), kspec(dv.shape[2]), kspec(gk.shape[2]), kspec(gv.shape[2]),
            pl.BlockSpec(lam_vecs.shape, const2),
            pl.BlockSpec(subln.shape, const2),
            pl.BlockSpec(woab.shape, const2),
            pl.BlockSpec(woc.shape, const2),
        ],
        out_specs=qspec(d),
        out_shape=jax.ShapeDtypeStruct((b, l, d), F32),
        scratch_shapes=[pltpu.VMEM((tq, gq.shape[2]), BF16)],
        compiler_params=_params(2),
        name="attn",
    )(x3, mod_l, oa, dq, gq, dk, dv, gk, gv, lam_vecs, subln, woab, woc)


def _top_rows(s, k, rows_ref):
    cur = s
    rank = jnp.full(s.shape, float(k), F32)
    for r in range(k):
        mx = jnp.max(cur, axis=0, keepdims=True)
        rows_ref[r:r + 1, :] = mx
        hit = cur == mx
        rank = jnp.where(hit, float(r), rank)
        if r + 1 < k:
            cur = jnp.where(hit, -jnp.inf, cur)
    return rank


def _pair_words(x):
    u = pltpu.bitcast(x.astype(BF16).astype(F32), jnp.uint32)
    return u | lax.shift_right_logical(u, jnp.uint32(16))


def _kth_largest(c, k):
    cur = c
    for _ in range(k - 1):
        mx = jnp.max(cur, axis=0, keepdims=True)
        cur = jnp.where(cur == mx, -jnp.inf, cur)
    return jnp.max(cur, axis=0, keepdims=True)


def _route_head(hd, s1, s2, v1_ref, v2_ref, r2_ref, bb_ref, nn_ref, aa_ref, cols):
    _top_rows(s1, PEER_TOPK, v1_ref)
    rank2 = _top_rows(s2, PEER_TOPK, v2_ref)
    v1 = v1_ref[...]
    v2 = v2_ref[...]
    cands = [v1[0:1] + v2]
    cands += [v1[r:r + 1] + v2[0:8] for r in range(1, PEER_TOPK)]
    cand = jnp.concatenate(cands, axis=0)
    tau = _kth_largest(cand, PEER_TOPK)
    top = v1[0:1] + v2[0:1]
    z = jnp.sum(jnp.where(cand >= tau, jnp.exp(cand - top), 0.0), axis=0, keepdims=True)
    n = jnp.zeros(s1.shape, F32)
    for k in range(PEER_TOPK):
        n = n + jnp.where(s1 + v2[k:k + 1] >= tau, 1.0, 0.0)
    a = jnp.exp(s1 - v1[0:1]) * (1.0 / z)
    r2_ref[hd, :, cols] = rank2.astype(BF16)
    bb_ref[hd, :, cols] = jnp.exp(s2 - v2[0:1]).astype(BF16)
    n_words = _pair_words(n)
    a_words = _pair_words(a)
    for k in range(N_KEYS // PEER_KEY_ROWS):
        nn_ref[hd, k, :, cols] = n_words[k * PEER_KEY_ROWS:(k + 1) * PEER_KEY_ROWS]
        aa_ref[hd, k, :, cols] = a_words[k * PEER_KEY_ROWS:(k + 1) * PEER_KEY_ROWS]


def _peer_kernel(x_ref, mod_ref, g2_ref, wqt_ref, sk_ref, u_ref, vt_ref, fin_ref, out_ref,
                 hb_ref, r2_ref, bb_ref, nn_ref, aa_ref, v1_ref, v2_ref, pre_ref, wt_ref, acc_ref,
                 *, final_norm):
    c = pl.program_id(1)
    t = x_ref.shape[0]

    @pl.when(c == 0)
    def _routing():
        m = mod_ref[0]
        h = _rms(x_ref[...]) * g2_ref[...] * (1.0 + m[4:5]) + m[3:4]
        hb = h.astype(BF16)
        hb_ref[...] = hb
        qt = _dot_nt(wqt_ref[...], hb)
        for hd in range(PEER_HEADS):
            base = hd * 2 * PEER_HALF
            s1 = _dot(sk_ref[0], qt[base:base + PEER_HALF].astype(BF16))
            s2 = _dot(sk_ref[1], qt[base + PEER_HALF:base + 2 * PEER_HALF].astype(BF16))
            for lt in range(t // LANES):
                cols = slice(lt * LANES, (lt + 1) * LANES)
                _route_head(hd, s1[:, cols], s2[:, cols], v1_ref, v2_ref, r2_ref, bb_ref, nn_ref, aa_ref, cols)
        acc_ref[...] = jnp.zeros(acc_ref.shape, F32)

    pre_ref[...] = _dot_nt(u_ref[...], hb_ref[...])

    sub = BF16_ROWS
    for r in range(PEER_KEY_ROWS):
        for lt in range(t // LANES):
            cols = slice(lt * LANES, (lt + 1) * LANES)
            n_b, a_b = [], []
            for hd in range(PEER_HEADS):
                n_row = jnp.broadcast_to(nn_ref[hd, c, r:r + 1, cols], (sub // 2, LANES))
                a_row = jnp.broadcast_to(aa_ref[hd, c, r:r + 1, cols], (sub // 2, LANES))
                n_b.append(pltpu.bitcast(n_row, BF16))
                a_b.append(pltpu.bitcast(a_row, BF16))
            for jb in range(N_KEYS // sub):
                keys = slice(jb * sub, (jb + 1) * sub)
                g = jnp.zeros((sub, LANES), BF16)
                for hd in range(PEER_HEADS):
                    kept = jnp.where(r2_ref[hd, keys, cols] < n_b[hd], bb_ref[hd, keys, cols],
                                     jnp.zeros((sub, LANES), BF16))
                    g = g + kept * a_b[hd]
                rows = slice(r * N_KEYS + jb * sub, r * N_KEYS + (jb + 1) * sub)
                wt_ref[rows, cols] = _gelu_folded(pre_ref[rows, cols].astype(BF16)) * g
    acc_ref[...] += _dot(vt_ref[...], wt_ref[...])

    @pl.when(c == pl.num_programs(1) - 1)
    def _finish():
        y = x_ref[...] + mod_ref[0][5:6] * acc_ref[...].T
        if final_norm:
            y = _rms(y) * fin_ref[...]
        out_ref[...] = y


def _peer(x, mod_l, row_of_block, g2, wqt, sk, u, vt, fin, final_norm):
    n, d = x.shape
    t = PEER_ROWS
    ec = PEER_KEY_ROWS * N_KEYS
    nc = u.shape[0] // ec
    const2 = lambda i, c: (0, 0)
    gate = pltpu.VMEM((PEER_HEADS, N_KEYS, t), BF16)
    gate_rows = pltpu.VMEM((PEER_HEADS, N_KEYS // PEER_KEY_ROWS, PEER_KEY_ROWS, t), jnp.uint32)
    return pl.pallas_call(
        functools.partial(_peer_kernel, final_norm=final_norm),
        grid=(n // t, nc),
        in_specs=[
            pl.BlockSpec((t, d), lambda i, c: (i, 0)),
            pl.BlockSpec((1, ADA_CHUNKS, d), lambda i, c: (row_of_block(i), 0, 0)),
            pl.BlockSpec((1, d), const2),
            pl.BlockSpec(wqt.shape, const2),
            pl.BlockSpec(sk.shape, lambda i, c: (0, 0, 0)),
            pl.BlockSpec((ec, d), lambda i, c: (c, 0)),
            pl.BlockSpec((None, d, ec), lambda i, c: (c, 0, 0)),
            pl.BlockSpec((1, d), const2),
        ],
        out_specs=pl.BlockSpec((t, d), lambda i, c: (i, 0)),
        out_shape=jax.ShapeDtypeStruct((n, d), F32),
        scratch_shapes=[
            pltpu.VMEM((t, d), BF16),
            gate, gate, gate_rows, gate_rows,
            pltpu.VMEM((PEER_TOPK, LANES), F32),
            pltpu.VMEM((PEER_TOPK, LANES), F32),
            pltpu.VMEM((ec, t), F32),
            pltpu.VMEM((ec, t), BF16),
            pltpu.VMEM((d, t), F32),
        ],
        compiler_params=_params(2),
        name="peer",
    )(x, mod_l, g2, wqt, sk, u, vt, fin)


def _rope_tables(n_tokens, dim):
    rows = n_tokens // GRID_W
    row = jnp.repeat(jnp.arange(rows, dtype=F32), GRID_W)
    col = jnp.tile(jnp.arange(GRID_W, dtype=F32), rows)
    quarter = dim // 4
    inv = ROPE_THETA ** (-jnp.arange(quarter, dtype=F32) / quarter)
    ang = jnp.concatenate([row[:, None] * inv, col[:, None] * inv], axis=-1)
    cos, sin = jnp.cos(ang), jnp.sin(ang)
    reps = LANES // dim
    return (jnp.tile(jnp.concatenate([cos, cos], axis=-1), (1, reps)),
            jnp.tile(jnp.concatenate([-sin, sin], axis=-1), (1, reps)))


def _slot_pad(a, axis, n_heads):
    parts = []
    for hq in range(n_heads):
        blk = lax.slice_in_dim(a, hq * HEAD_DIM, (hq + 1) * HEAD_DIM, axis=axis)
        zero = jnp.zeros_like(blk)
        parts += [blk, zero] if (hq // GQA_GROUP) == 0 else [zero, blk]
    return jnp.concatenate(parts, axis=axis)


def kernel(x_prompt, x_sample, cache_diff_k, cache_diff_v, cache_gqa_k, cache_gqa_v, c, c_ctx, ada_w, ada_b, norm1_g, norm2_g, w_in, sgu_norm_g, sgu_w, sgu_b, diff_lq1, diff_lk1, diff_lq2, diff_lk2, diff_subln_g, gqa_qnorm_g, gqa_knorm_g, w_out, peer_wq, peer_subkeys, peer_u, peer_v, final_g):
    depth, d, _ = w_in.shape
    bp, lp, _ = x_prompt.shape
    bs, ls, _ = x_sample.shape
    past = cache_diff_k.shape[2]
    sgu_cols = sgu_norm_g.shape[1]
    diff_cols = diff_subln_g.shape[1] * cache_diff_v.shape[3]
    kv_cols = cache_gqa_k.shape[3] * HEAD_DIM
    gq_cols = kv_cols * GQA_GROUP
    n_gqa_heads = gq_cols // HEAD_DIM
    assert kv_cols == LANES and diff_cols == 2 * LANES and sgu_cols == 2 * LANES

    n_rows = -(-(1 + bs) // 8) * 8
    cvecs = jnp.zeros((n_rows, d), F32).at[0].set(c_ctx).at[1:1 + bs].set(c)
    mod = _modulation(cvecs, ada_w, ada_b).reshape(depth, n_rows, ADA_CHUNKS, d)

    o = 2 * sgu_cols + 3 * diff_cols
    win_ext = jnp.concatenate(
        [w_in[:, :, :o], _slot_pad(w_in[:, :, o:o + gq_cols], 2, n_gqa_heads), w_in[:, :, o + gq_cols:]],
        axis=2).astype(BF16)
    woab = w_out[:, :sgu_cols + diff_cols].astype(BF16)
    woc = _slot_pad(w_out[:, sgu_cols + diff_cols:], 1, n_gqa_heads).astype(BF16)
    sgw = sgu_w.astype(BF16)
    sgb = jnp.repeat(jnp.swapaxes(sgu_b, 1, 2), HEAD_DIM, axis=2)
    qng = jnp.tile(gqa_qnorm_g, (1, 2))[:, None, :]
    kng = jnp.tile(gqa_knorm_g, (1, 2))[:, None, :]
    subln = jnp.tile(diff_subln_g, (1, diff_cols // HEAD_DIM))[:, None, :]
    lam_vecs = jnp.stack([diff_lq1, diff_lk1, diff_lq2, diff_lk2], axis=1)
    wqt = jnp.swapaxes(peer_wq, 1, 2).astype(BF16)
    sk = peer_subkeys.astype(BF16)
    u_tab = peer_u.astype(BF16)
    ec = PEER_KEY_ROWS * N_KEYS
    vt_tab = jnp.swapaxes(peer_v.astype(BF16).reshape(depth, -1, ec, d), 2, 3)
    tabs = _rope_tables(ls, DIFF_DQK) + _rope_tables(ls, HEAD_DIM)

    def run(x3, sample):
        b, l, _ = x3.shape
        x = x3.reshape(b * l, d)
        if sample:
            row_front = lambda i: 1 + i // (l // FRONT_ROWS)
            row_attn = lambda i: 1 + i
            row_peer = lambda i: 1 + i // (l // PEER_ROWS)
        else:
            row_front = row_attn = row_peer = lambda i: 0
        kv_out = []
        for layer in range(depth):
            lam_init = 0.8 - 0.6 * math.exp(-0.3 * layer)
            mod_l = mod[layer]
            outs = _front(x, mod_l, row_front, norm1_g[layer][None], win_ext[layer], sgu_norm_g[layer][None],
                          sgw[layer], sgb[layer], qng[layer], kng[layer], tabs if sample else None, not sample)
            oa, dq, dk, dv, gq, gk, gv = [a.reshape(b, l, a.shape[1]) for a in outs[:7]]
            if sample:
                cat = lambda new, old: jnp.concatenate(
                    [new, old[:, layer].reshape(b, past, -1).astype(BF16)], axis=1)
                dk, dv = cat(dk, cache_diff_k), cat(dv, cache_diff_v)
                gk, gv = cat(gk, cache_gqa_k), cat(gv, cache_gqa_v)
            else:
                kv_out.append(outs[7:])
            x3n = _attn(x.reshape(b, l, d), mod_l, row_attn, oa, dq, gq, dk, dv, gk, gv,
                        lam_vecs[layer], subln[layer], woab[layer], woc[layer], lam_init)
            x = _peer(x3n.reshape(b * l, d), mod_l, row_peer, norm2_g[layer][None], wqt[layer], sk[layer],
                      u_tab[layer], vt_tab[layer], final_g[None], layer == depth - 1)
        return x.reshape(b, l, d), kv_out

    y_prompt, kv = run(x_prompt, False)
    y_sample, _ = run(x_sample, True)
    stack = lambda idx, shape: jnp.stack([kv[layer][idx].reshape(shape) for layer in range(depth)], axis=1)
    new_diff_k = stack(0, (bp, lp) + cache_diff_k.shape[3:])
    new_diff_v = stack(1, (bp, lp) + cache_diff_v.shape[3:])
    new_gqa_k = stack(2, (bp, lp) + cache_gqa_k.shape[3:])
    new_gqa_v = stack(3, (bp, lp) + cache_gqa_v.shape[3:])
    return (y_prompt, y_sample, new_diff_k, new_diff_v, new_gqa_k, new_gqa_v)
```

```python
import functools
import math

import jax
import jax.numpy as jnp
from jax import lax
from jax.experimental import pallas as pl
from jax.experimental.pallas import tpu as pltpu

F32 = jnp.float32
BF16 = jnp.bfloat16

HEAD_DIM = 64
SGU_CHUNK = 128
DIFF_DQK = 32
GQA_GROUP = 4
ADA_CHUNKS = 6
N_KEYS = 128
PEER_HEADS = 8
PEER_TOPK = 16
PEER_HALF = 64
GRID_W = 64
ROPE_THETA = 10000.0
EPS = 1e-6
DIFF_SCALE = DIFF_DQK ** -0.5
GQA_SCALE = HEAD_DIM ** -0.5

LANES = 128
BF16_ROWS = 16
VMEM_LIMIT_BYTES = 56 * 1024 * 1024

FRONT_ROWS = 256
ATTN_ROWS = 256
PEER_ROWS = 512
PEER_KEY_ROWS = 16


def _dot(a, b):
    return jnp.dot(a, b, preferred_element_type=F32)


def _dot_nt(a, b):
    return lax.dot_general(a, b, (((1,), (1,)), ((), ())), preferred_element_type=F32)


def _rms(x):
    return x * lax.rsqrt(jnp.mean(x * x, axis=-1, keepdims=True) + EPS)


def _gelu(x):
    c = math.sqrt(2.0 / math.pi)
    return 0.5 * x * (1.0 + jnp.tanh(c * (x + 0.044715 * (x * x * x))))


def _gelu_folded(x):
    c = math.sqrt(2.0 / math.pi)
    hx = 0.5 * x
    return hx + hx * jnp.tanh(x * (c + (0.044715 * c) * (x * x)))


def _params(n_axes, flags=None):
    return pltpu.CompilerParams(
        dimension_semantics=("arbitrary",) * n_axes, vmem_limit_bytes=VMEM_LIMIT_BYTES, flags=flags)


def _mod_kernel(c_ref, w_ref, b_ref, o_ref):
    c = c_ref[...]
    s = c / (1.0 + jnp.exp(-c))
    o_ref[0] = _dot(s.astype(BF16), w_ref[0].astype(BF16)) + b_ref[0]


def _modulation(cvecs, ada_w, ada_b):
    depth, d, n = ada_w.shape
    rows = cvecs.shape[0]
    tn = n // 4
    return pl.pallas_call(
        _mod_kernel,
        grid=(depth, n // tn),
        in_specs=[
            pl.BlockSpec((rows, d), lambda l, j: (0, 0)),
            pl.BlockSpec((1, d, tn), lambda l, j: (l, 0, j)),
            pl.BlockSpec((1, 1, tn), lambda l, j: (l, 0, j)),
        ],
        out_specs=pl.BlockSpec((1, rows, tn), lambda l, j: (l, 0, j)),
        out_shape=jax.ShapeDtypeStruct((depth, rows, n), F32),
        compiler_params=_params(2),
        name="modulation",
    )(cvecs, ada_w, ada_b.reshape(depth, 1, n))


def _rope(x, cos, sin_signed, half):
    lane = lax.broadcasted_iota(jnp.int32, (1, LANES), 1)
    first = (lane % (2 * half)) < half
    partner = jnp.where(first, pltpu.roll(x, LANES - half, 1), pltpu.roll(x, half, 1))
    return x * cos + partner * sin_signed


def _front_kernel(*refs, rope, emit_kv, d_model, sgu_w_cols):
    (x_ref, mod_ref, g1_ref, win_ref, sgn_ref, sgw_ref, sgb_ref, qng_ref, kng_ref) = refs[:9]
    pos = 9
    if rope:
        cd_ref, sd_ref, cg_ref, sg_ref = refs[pos:pos + 4]
        pos += 4
    oa_ref, dq_ref, dk_ref, dv_ref, gq_ref, gk_ref, gv_ref = refs[pos:pos + 7]
    pos += 7
    if emit_kv:
        kd32_ref, vd32_ref, kg32_ref, vg32_ref = refs[pos:pos + 4]

    x = x_ref[...]
    m = mod_ref[0]
    h = _rms(x) * g1_ref[...] * (1.0 + m[1:2]) + m[0:1]
    proj = _dot(h.astype(BF16), win_ref[...])

    w = sgu_w_cols
    a_u = _gelu(proj[:, 0:w])
    a_v = _rms(_gelu(proj[:, w:2 * w])) * sgn_ref[...]
    a_vb = a_v.astype(BF16)
    lane_w = lax.broadcasted_iota(jnp.int32, (1, w), 1)
    n_groups = w // HEAD_DIM
    t = x.shape[0]
    for ch in range(t // SGU_CHUNK):
        rows = slice(ch * SGU_CHUNK, (ch + 1) * SGU_CHUNK)
        av = a_vb[rows]
        mix = jnp.zeros((SGU_CHUNK, w), F32)
        for g in range(n_groups):
            part = _dot(sgw_ref[g], av)
            mix = jnp.where(lane_w // HEAD_DIM == g, part, mix)
        oa_ref[rows, :] = (a_u[rows] * (mix + sgb_ref[...])).astype(BF16)

    o = 2 * w
    d_q = proj[:, o:o + 256]
    d_k = proj[:, o + 256:o + 512]
    d_v = proj[:, o + 512:o + 768]
    g_q = proj[:, o + 768:o + 1792]
    g_k = proj[:, o + 1792:o + 1920]
    g_v = proj[:, o + 1920:o + 2048]

    lane = lax.broadcasted_iota(jnp.int32, (1, LANES), 1)
    low = lane < HEAD_DIM
    k2 = g_k * g_k
    s_low = jnp.sum(jnp.where(low, k2, 0.0), axis=-1, keepdims=True)
    s_all = jnp.sum(k2, axis=-1, keepdims=True)
    ms = jnp.where(low, s_low, s_all - s_low) * (1.0 / HEAD_DIM)
    g_k = g_k * lax.rsqrt(ms + EPS) * kng_ref[...]

    if emit_kv:
        kd32_ref[...] = d_k
        vd32_ref[...] = d_v
        kg32_ref[...] = g_k
        vg32_ref[...] = g_v

    if rope:
        g_k = _rope(g_k, cg_ref[...], sg_ref[...], HEAD_DIM // 2)
    gk_ref[...] = g_k.astype(BF16)
    gv_ref[...] = g_v.astype(BF16)
    dv_ref[...] = d_v.astype(BF16)

    for s in range(2):
        cols = slice(s * LANES, (s + 1) * LANES)
        q = d_q[:, cols]
        k = d_k[:, cols]
        if rope:
            q = _rope(q, cd_ref[...], sd_ref[...], DIFF_DQK // 2)
            k = _rope(k, cd_ref[...], sd_ref[...], DIFF_DQK // 2)
        dq_ref[:, cols] = (q * DIFF_SCALE).astype(BF16)
        dk_ref[:, cols] = k.astype(BF16)

    for s in range(g_q.shape[1] // LANES):
        cols = slice(s * LANES, (s + 1) * LANES)
        q = g_q[:, cols]
        ms = jnp.sum(q * q, axis=-1, keepdims=True) * (1.0 / HEAD_DIM)
        q = q * lax.rsqrt(ms + EPS) * qng_ref[...]
        if rope:
            q = _rope(q, cg_ref[...], sg_ref[...], HEAD_DIM // 2)
        gq_ref[:, cols] = (q * GQA_SCALE).astype(BF16)


def _front(x, mod_l, row_of_block, g1, win, sgn, sgw, sgb, qng, kng, rope_tabs, emit_kv):
    n, d = x.shape
    t = FRONT_ROWS
    nb = n // t
    rope = rope_tabs is not None
    const2 = lambda i: (0, 0)
    in_specs = [
        pl.BlockSpec((t, d), lambda i: (i, 0)),
        pl.BlockSpec((1, ADA_CHUNKS, d), lambda i: (row_of_block(i), 0, 0)),
        pl.BlockSpec((1, d), const2),
        pl.BlockSpec(win.shape, const2),
        pl.BlockSpec(sgn.shape, const2),
        pl.BlockSpec(sgw.shape, lambda i: (0, 0, 0)),
        pl.BlockSpec(sgb.shape, const2),
        pl.BlockSpec(qng.shape, const2),
        pl.BlockSpec(kng.shape, const2),
    ]
    args = [x, mod_l, g1, win, sgn, sgw, sgb, qng, kng]
    if rope:
        blocks_per_seq = rope_tabs[0].shape[0] // t
        for tab in rope_tabs:
            in_specs.append(pl.BlockSpec((t, LANES), lambda i: (i % blocks_per_seq, 0)))
            args.append(tab)
    widths = [(256, BF16), (256, BF16), (256, BF16), (256, BF16), (1024, BF16), (128, BF16), (128, BF16)]
    if emit_kv:
        widths += [(256, F32), (256, F32), (128, F32), (128, F32)]
    out_specs = [pl.BlockSpec((t, wd), lambda i: (i, 0)) for wd, _ in widths]
    out_shape = [jax.ShapeDtypeStruct((n, wd), dt) for wd, dt in widths]
    return pl.pallas_call(
        functools.partial(_front_kernel, rope=rope, emit_kv=emit_kv, d_model=d, sgu_w_cols=sgn.shape[1]),
        grid=(nb,),
        in_specs=in_specs,
        out_specs=out_specs,
        out_shape=out_shape,
        compiler_params=_params(1),
        name="front",
    )(*args)


def _softmax_pv(s, v):
    mx = jnp.max(s, axis=-1, keepdims=True)
    e = jnp.exp(s - mx)
    z = jnp.sum(e, axis=-1, keepdims=True)
    return _dot(e.astype(BF16), v) * (1.0 / z)


def _attn_kernel(x_ref, mod_ref, oa_ref, dq_ref, gq_ref, dk_ref, dv_ref, gk_ref, gv_ref,
                 lam_ref, subln_ref, woab_ref, woc_ref, out_ref, oc_ref, *, lam_init):
    lam = (jnp.exp(jnp.sum(lam_ref[0:1] * lam_ref[1:2], axis=-1, keepdims=True))
           - jnp.exp(jnp.sum(lam_ref[2:3] * lam_ref[3:4], axis=-1, keepdims=True)) + lam_init)

    dq = dq_ref[...]
    dk = dk_ref[...]
    dv = dv_ref[...]
    w = dq.shape[1]
    lane = lax.broadcasted_iota(jnp.int32, (1, w), 1)
    out_b = jnp.zeros(dq.shape, F32)
    inv = jnp.zeros(dq.shape, F32)
    n_heads = w // HEAD_DIM
    for hd in range(n_heads):
        parts = []
        for xx in range(2):
            sel = (lane // DIFF_DQK) == (2 * hd + xx)
            qm = jnp.where(sel, dq, jnp.zeros_like(dq))
            parts.append(_softmax_pv(_dot_nt(qm, dk), dv))
        o = parts[0] - lam * parts[1]
        in_head = (lane // HEAD_DIM) == hd
        ms = jnp.sum(jnp.where(in_head, o * o, 0.0), axis=-1, keepdims=True) * (1.0 / HEAD_DIM)
        out_b = jnp.where(in_head, o, out_b)
        inv = jnp.where(in_head, lax.rsqrt(ms + EPS), inv)
    out_b = out_b * inv * subln_ref[...] * (1.0 - lam_init)

    gk = gk_ref[...]
    gv = gv_ref[...]
    for hq in range(gq_ref.shape[1] // LANES):
        cols = slice(hq * LANES, (hq + 1) * LANES)
        oc_ref[:, cols] = _softmax_pv(_dot_nt(gq_ref[:, cols], gk), gv).astype(BF16)

    ab = jnp.concatenate([oa_ref[...], out_b.astype(BF16)], axis=-1)
    mix = _dot(ab, woab_ref[...]) + _dot(oc_ref[...], woc_ref[...])
    out_ref[...] = x_ref[...] + mod_ref[0][2:3] * mix


def _attn(x3, mod_l, row_of_batch, oa, dq, gq, dk, dv, gk, gv, lam_vecs, subln, woab, woc, lam_init):
    b, l, d = x3.shape
    tq = min(ATTN_ROWS, l)
    lk = dk.shape[1]
    qspec = lambda wd: pl.BlockSpec((None, tq, wd), lambda i, j: (i, j, 0))
    kspec = lambda wd: pl.BlockSpec((None, lk, wd), lambda i, j: (i, 0, 0))
    const2 = lambda i, j: (0, 0)
    return pl.pallas_call(
        functools.partial(_attn_kernel, lam_init=lam_init),
        grid=(b, l // tq),
        in_specs=[
            qspec(d),
            pl.BlockSpec((1, ADA_CHUNKS, d), lambda i, j: (row_of_batch(i), 0, 0)),
            qspec(oa.shape[2]), qspec(dq.shape[2]), qspec(gq.shape[2]),
            kspec(dk.shape[2]), kspec(dv.shape[2]), kspec(gk.shape[2]), kspec(gv.shape[2]),
            pl.BlockSpec(lam_vecs.shape, const2),
            pl.BlockSpec(subln.shape, const2),
            pl.BlockSpec(woab.shape, const2),
            pl.BlockSpec(woc.shape, const2),
        ],
        out_specs=qspec(d),
        out_shape=jax.ShapeDtypeStruct((b, l, d), F32),
        scratch_shapes=[pltpu.VMEM((tq, gq.shape[2]), BF16)],
        compiler_params=_params(2),
        name="attn",
    )(x3, mod_l, oa, dq, gq, dk, dv, gk, gv, lam_vecs, subln, woab, woc)


def _top_rows(s, k, rows_ref):
    cur = s
    rank = jnp.full(s.shape, float(k), F32)
    for r in range(k):
        mx = jnp.max(cur, axis=0, keepdims=True)
        rows_ref[r:r + 1, :] = mx
        hit = cur == mx
        rank = jnp.where(hit, float(r), rank)
        if r + 1 < k:
            cur = jnp.where(hit, -jnp.inf, cur)
    return rank


def _pair_words(x):
    u = pltpu.bitcast(x.astype(BF16).astype(F32), jnp.uint32)
    return u | lax.shift_right_logical(u, jnp.uint32(16))


def _kth_largest(c, k):
    cur = c
    for _ in range(k - 1):
        mx = jnp.max(cur, axis=0, keepdims=True)
        cur = jnp.where(cur == mx, -jnp.inf, cur)
    return jnp.max(cur, axis=0, keepdims=True)


def _route_head(hd, s1, s2, v1_ref, v2_ref, r2_ref, bb_ref, nn_ref, aa_ref, cols):
    _top_rows(s1, PEER_TOPK, v1_ref)
    rank2 = _top_rows(s2, PEER_TOPK, v2_ref)
    v1 = v1_ref[...]
    v2 = v2_ref[...]
    cands = [v1[0:1] + v2]
    cands += [v1[r:r + 1] + v2[0:8] for r in range(1, PEER_TOPK)]
    cand = jnp.concatenate(cands, axis=0)
    tau = _kth_largest(cand, PEER_TOPK)
    top = v1[0:1] + v2[0:1]
    z = jnp.sum(jnp.where(cand >= tau, jnp.exp(cand - top), 0.0), axis=0, keepdims=True)
    n = jnp.zeros(s1.shape, F32)
    for k in range(PEER_TOPK):
        n = n + jnp.where(s1 + v2[k:k + 1] >= tau, 1.0, 0.0)
    a = jnp.exp(s1 - v1[0:1]) * (1.0 / z)
    r2_ref[hd, :, cols] = rank2.astype(BF16)
    bb_ref[hd, :, cols] = jnp.exp(s2 - v2[0:1]).astype(BF16)
    n_words = _pair_words(n)
    a_words = _pair_words(a)
    for k in range(N_KEYS // PEER_KEY_ROWS):
        nn_ref[hd, k, :, cols] = n_words[k * PEER_KEY_ROWS:(k + 1) * PEER_KEY_ROWS]
        aa_ref[hd, k, :, cols] = a_words[k * PEER_KEY_ROWS:(k + 1) * PEER_KEY_ROWS]


def _peer_kernel(x_ref, mod_ref, g2_ref, wqt_ref, sk_ref, u_ref, vt_ref, fin_ref, out_ref,
                 hb_ref, r2_ref, bb_ref, nn_ref, aa_ref, v1_ref, v2_ref, pre_ref, wt_ref, acc_ref,
                 *, final_norm):
    c = pl.program_id(1)
    t = x_ref.shape[0]

    @pl.when(c == 0)
    def _routing():
        m = mod_ref[0]
        h = _rms(x_ref[...]) * g2_ref[...] * (1.0 + m[4:5]) + m[3:4]
        hb = h.astype(BF16)
        hb_ref[...] = hb
        qt = _dot_nt(wqt_ref[...], hb)
        for hd in range(PEER_HEADS):
            base = hd * 2 * PEER_HALF
            s1 = _dot(sk_ref[0], qt[base:base + PEER_HALF].astype(BF16))
            s2 = _dot(sk_ref[1], qt[base + PEER_HALF:base + 2 * PEER_HALF].astype(BF16))
            for lt in range(t // LANES):
                cols = slice(lt * LANES, (lt + 1) * LANES)
                _route_head(hd, s1[:, cols], s2[:, cols], v1_ref, v2_ref, r2_ref, bb_ref, nn_ref, aa_ref, cols)
        acc_ref[...] = jnp.zeros(acc_ref.shape, F32)

    pre_ref[...] = _dot_nt(u_ref[...], hb_ref[...])

    sub = BF16_ROWS
    for r in range(PEER_KEY_ROWS):
        for lt in range(t // LANES):
            cols = slice(lt * LANES, (lt + 1) * LANES)
            n_b, a_b = [], []
            for hd in range(PEER_HEADS):
                n_row = jnp.broadcast_to(nn_ref[hd, c, r:r + 1, cols], (sub // 2, LANES))
                a_row = jnp.broadcast_to(aa_ref[hd, c, r:r + 1, cols], (sub // 2, LANES))
                n_b.append(pltpu.bitcast(n_row, BF16))
                a_b.append(pltpu.bitcast(a_row, BF16))
            for jb in range(N_KEYS // sub):
                keys = slice(jb * sub, (jb + 1) * sub)
                g = jnp.zeros((sub, LANES), BF16)
                for hd in range(PEER_HEADS):
                    kept = jnp.where(r2_ref[hd, keys, cols] < n_b[hd], bb_ref[hd, keys, cols],
                                     jnp.zeros((sub, LANES), BF16))
                    g = g + kept * a_b[hd]
                rows = slice(r * N_KEYS + jb * sub, r * N_KEYS + (jb + 1) * sub)
                wt_ref[rows, cols] = _gelu_folded(pre_ref[rows, cols].astype(BF16)) * g
    acc_ref[...] += _dot(vt_ref[...], wt_ref[...])

    @pl.when(c == pl.num_programs(1) - 1)
    def _finish():
        y = x_ref[...] + mod_ref[0][5:6] * acc_ref[...].T
        if final_norm:
            y = _rms(y) * fin_ref[...]
        out_ref[...] = y


def _peer(x, mod_l, row_of_block, g2, wqt, sk, u, vt, fin, final_norm):
    n, d = x.shape
    t = PEER_ROWS
    ec = PEER_KEY_ROWS * N_KEYS
    nc = u.shape[0] // ec
    const2 = lambda i, c: (0, 0)
    gate = pltpu.VMEM((PEER_HEADS, N_KEYS, t), BF16)
    gate_rows = pltpu.VMEM((PEER_HEADS, N_KEYS // PEER_KEY_ROWS, PEER_KEY_ROWS, t), jnp.uint32)
    return pl.pallas_call(
        functools.partial(_peer_kernel, final_norm=final_norm),
        grid=(n // t, nc),
        in_specs=[
            pl.BlockSpec((t, d), lambda i, c: (i, 0)),
            pl.BlockSpec((1, ADA_CHUNKS, d), lambda i, c: (row_of_block(i), 0, 0)),
            pl.BlockSpec((1, d), const2),
            pl.BlockSpec(wqt.shape, const2),
            pl.BlockSpec(sk.shape, lambda i, c: (0, 0, 0)),
            pl.BlockSpec((ec, d), lambda i, c: (c, 0)),
            pl.BlockSpec((None, d, ec), lambda i, c: (c, 0, 0)),
            pl.BlockSpec((1, d), const2),
        ],
        out_specs=pl.BlockSpec((t, d), lambda i, c: (i, 0)),
        out_shape=jax.ShapeDtypeStruct((n, d), F32),
        scratch_shapes=[
            pltpu.VMEM((t, d), BF16),
            gate, gate, gate_rows, gate_rows,
            pltpu.VMEM((PEER_TOPK, LANES), F32),
            pltpu.VMEM((PEER_TOPK, LANES), F32),
            pltpu.VMEM((ec, t), F32),
            pltpu.VMEM((ec, t), BF16),
            pltpu.VMEM((d, t), F32),
        ],
        compiler_params=_params(2),
        name="peer",
    )(x, mod_l, g2, wqt, sk, u, vt, fin)


def _rope_tables(n_tokens, dim):
    rows = n_tokens // GRID_W
    row = jnp.repeat(jnp.arange(rows, dtype=F32), GRID_W)
    col = jnp.tile(jnp.arange(GRID_W, dtype=F32), rows)
    quarter = dim // 4
    inv = ROPE_THETA ** (-jnp.arange(quarter, dtype=F32) / quarter)
    ang = jnp.concatenate([row[:, None] * inv, col[:, None] * inv], axis=-1)
    cos, sin = jnp.cos(ang), jnp.sin(ang)
    reps = LANES // dim
    return (jnp.tile(jnp.concatenate([cos, cos], axis=-1), (1, reps)),
            jnp.tile(jnp.concatenate([-sin, sin], axis=-1), (1, reps)))


def _slot_pad(a, axis, n_heads):
    parts = []
    for hq in range(n_heads):
        blk = lax.slice_in_dim(a, hq * HEAD_DIM, (hq + 1) * HEAD_DIM, axis=axis)
        zero = jnp.zeros_like(blk)
        parts += [blk, zero] if (hq // GQA_GROUP) == 0 else [zero, blk]
    return jnp.concatenate(parts, axis=axis)


def kernel(x_prompt, x_sample, cache_diff_k, cache_diff_v, cache_gqa_k, cache_gqa_v, c, c_ctx, ada_w, ada_b, norm1_g, norm2_g, w_in, sgu_norm_g, sgu_w, sgu_b, diff_lq1, diff_lk1, diff_lq2, diff_lk2, diff_subln_g, gqa_qnorm_g, gqa_knorm_g, w_out, peer_wq, peer_subkeys, peer_u, peer_v, final_g):
    depth, d, _ = w_in.shape
    bp, lp, _ = x_prompt.shape
    bs, ls, _ = x_sample.shape
    past = cache_diff_k.shape[2]
    sgu_cols = sgu_norm_g.shape[1]
    diff_cols = diff_subln_g.shape[1] * cache_diff_v.shape[3]
    kv_cols = cache_gqa_k.shape[3] * HEAD_DIM
    gq_cols = kv_cols * GQA_GROUP
    n_gqa_heads = gq_cols // HEAD_DIM
    assert kv_cols == LANES and diff_cols == 2 * LANES and sgu_cols == 2 * LANES

    n_rows = -(-(1 + bs) // 8) * 8
    cvecs = jnp.zeros((n_rows, d), F32).at[0].set(c_ctx).at[1:1 + bs].set(c)
    mod = _modulation(cvecs, ada_w, ada_b).reshape(depth, n_rows, ADA_CHUNKS, d)

    o = 2 * sgu_cols + 3 * diff_cols
    win_ext = jnp.concatenate(
        [w_in[:, :, :o], _slot_pad(w_in[:, :, o:o + gq_cols], 2, n_gqa_heads), w_in[:, :, o + gq_cols:]],
        axis=2).astype(BF16)
    woab = w_out[:, :sgu_cols + diff_cols].astype(BF16)
    woc = _slot_pad(w_out[:, sgu_cols + diff_cols:], 1, n_gqa_heads).astype(BF16)
    sgw = sgu_w.astype(BF16)
    sgb = jnp.repeat(jnp.swapaxes(sgu_b, 1, 2), HEAD_DIM, axis=2)
    qng = jnp.tile(gqa_qnorm_g, (1, 2))[:, None, :]
    kng = jnp.tile(gqa_knorm_g, (1, 2))[:, None, :]
    subln = jnp.tile(diff_subln_g, (1, diff_cols // HEAD_DIM))[:, None, :]
    lam_vecs = jnp.stack([diff_lq1, diff_lk1, diff_lq2, diff_lk2], axis=1)
    wqt = jnp.swapaxes(peer_wq, 1, 2).astype(BF16)
    sk = peer_subkeys.astype(BF16)
    u_tab = peer_u.astype(BF16)
    ec = PEER_KEY_ROWS * N_KEYS
    vt_tab = jnp.swapaxes(peer_v.astype(BF16).reshape(depth, -1, ec, d), 2, 3)
    tabs = _rope_tables(ls, DIFF_DQK) + _rope_tables(ls, HEAD_DIM)

    def run(x3, sample):
        b, l, _ = x3.shape
        x = x3.reshape(b * l, d)
        if sample:
            row_front = lambda i: 1 + i // (l // FRONT_ROWS)
            row_attn = lambda i: 1 + i
            row_peer = lambda i: 1 + i // (l // PEER_ROWS)
        else:
            row_front = row_attn = row_peer = lambda i: 0
        kv_out = []
        for layer in range(depth):
            lam_init = 0.8 - 0.6 * math.exp(-0.3 * layer)
            mod_l = mod[layer]
            outs = _front(x, mod_l, row_front, norm1_g[layer][None], win_ext[layer], sgu_norm_g[layer][None],
                          sgw[layer], sgb[layer], qng[layer], kng[layer], tabs if sample else None, not sample)
            oa, dq, dk, dv, gq, gk, gv = [a.reshape(b, l, a.shape[1]) for a in outs[:7]]
            if sample:
                cat = lambda new, old: jnp.concatenate(
                    [new, old[:, layer].reshape(b, past, -1).astype(BF16)], axis=1)
                dk, dv = cat(dk, cache_diff_k), cat(dv, cache_diff_v)
                gk, gv = cat(gk, cache_gqa_k), cat(gv, cache_gqa_v)
            else:
                kv_out.append(outs[7:])
            x3n = _attn(x.reshape(b, l, d), mod_l, row_attn, oa, dq, gq, dk, dv, gk, gv,
                        lam_vecs[layer], subln[layer], woab[layer], woc[layer], lam_init)
            x = _peer(x3n.reshape(b * l, d), mod_l, row_peer, norm2_g[layer][None], wqt[layer], sk[layer],
                      u_tab[layer], vt_tab[layer], final_g[None], layer == depth - 1)
        return x.reshape(b, l, d), kv_out

    y_prompt, kv = run(x_prompt, False)
    y_sample, _ = run(x_sample, True)
    stack = lambda idx, shape: jnp.stack([kv[layer][idx].reshape(shape) for layer in range(depth)], axis=1)
    new_diff_k = stack(0, (bp, lp) + cache_diff_k.shape[3:])
    new_diff_v = stack(1, (bp, lp) + cache_diff_v.shape[3:])
    new_gqa_k = stack(2, (bp, lp) + cache_gqa_k.shape[3:])
    new_gqa_v = stack(3, (bp, lp) + cache_gqa_v.shape[3:])
    return (y_prompt, y_sample, new_diff_k, new_diff_v, new_gqa_k, new_gqa_v)
```

```python
import functools
import math

import jax
import jax.numpy as jnp
from jax import lax
from jax.experimental import pallas as pl
from jax.experimental.pallas import tpu as pltpu

F32 = jnp.float32
BF16 = jnp.bfloat16

HEAD_DIM = 64
SGU_CHUNK = 128
DIFF_DQK = 32
GQA_GROUP = 4
ADA_CHUNKS = 6
N_KEYS = 128
PEER_HEADS = 8
PEER_TOPK = 16
PEER_HALF = 64
GRID_W = 64
ROPE_THETA = 10000.0
EPS = 1e-6
DIFF_SCALE = DIFF_DQK ** -0.5
GQA_SCALE = HEAD_DIM ** -0.5

LANES = 128
BF16_ROWS = 16
VMEM_LIMIT_BYTES = 56 * 1024 * 1024

FRONT_ROWS = 256
ATTN_ROWS = 256
PEER_ROWS = 512
PEER_KEY_ROWS = 16


def _dot(a, b):
    return jnp.dot(a, b, preferred_element_type=F32)


def _dot_nt(a, b):
    return lax.dot_general(a, b, (((1,), (1,)), ((), ())), preferred_element_type=F32)


def _rms(x):
    return x * lax.rsqrt(jnp.mean(x * x, axis=-1, keepdims=True) + EPS)


def _gelu(x):
    c = math.sqrt(2.0 / math.pi)
    return 0.5 * x * (1.0 + jnp.tanh(c * (x + 0.044715 * (x * x * x))))


def _gelu_folded(x):
    c = math.sqrt(2.0 / math.pi)
    hx = 0.5 * x
    return hx + hx * jnp.tanh(x * (c + (0.044715 * c) * (x * x)))


def _params(n_axes, flags=None):
    return pltpu.CompilerParams(
        dimension_semantics=("arbitrary",) * n_axes, vmem_limit_bytes=VMEM_LIMIT_BYTES, flags=flags)


def _mod_kernel(c_ref, w_ref, b_ref, o_ref):
    c = c_ref[...]
    s = c / (1.0 + jnp.exp(-c))
    o_ref[0] = _dot(s.astype(BF16), w_ref[0].astype(BF16)) + b_ref[0]


def _modulation(cvecs, ada_w, ada_b):
    depth, d, n = ada_w.shape
    rows = cvecs.shape[0]
    tn = n // 4
    return pl.pallas_call(
        _mod_kernel,
        grid=(depth, n // tn),
        in_specs=[
            pl.BlockSpec((rows, d), lambda l, j: (0, 0)),
            pl.BlockSpec((1, d, tn), lambda l, j: (l, 0, j)),
            pl.BlockSpec((1, 1, tn), lambda l, j: (l, 0, j)),
        ],
        out_specs=pl.BlockSpec((1, rows, tn), lambda l, j: (l, 0, j)),
        out_shape=jax.ShapeDtypeStruct((depth, rows, n), F32),
        compiler_params=_params(2),
        name="modulation",
    )(cvecs, ada_w, ada_b.reshape(depth, 1, n))


def _rope(x, cos, sin_signed, half):
    lane = lax.broadcasted_iota(jnp.int32, (1, LANES), 1)
    first = (lane % (2 * half)) < half
    partner = jnp.where(first, pltpu.roll(x, LANES - half, 1), pltpu.roll(x, half, 1))
    return x * cos + partner * sin_signed


def _front_kernel(*refs, rope, emit_kv, d_model, sgu_w_cols):
    (x_ref, mod_ref, g1_ref, win_ref, sgn_ref, sgw_ref, sgb_ref, qng_ref, kng_ref) = refs[:9]
    pos = 9
    if rope:
        cd_ref, sd_ref, cg_ref, sg_ref = refs[pos:pos + 4]
        pos += 4
    oa_ref, dq_ref, dk_ref, dv_ref, gq_ref, gk_ref, gv_ref = refs[pos:pos + 7]
    pos += 7
    if emit_kv:
        kd32_ref, vd32_ref, kg32_ref, vg32_ref = refs[pos:pos + 4]

    x = x_ref[...]
    m = mod_ref[0]
    h = _rms(x) * g1_ref[...] * (1.0 + m[1:2]) + m[0:1]
    proj = _dot(h.astype(BF16), win_ref[...])

    w = sgu_w_cols
    a_u = _gelu(proj[:, 0:w])
    a_v = _rms(_gelu(proj[:, w:2 * w])) * sgn_ref[...]
    a_vb = a_v.astype(BF16)
    lane_w = lax.broadcasted_iota(jnp.int32, (1, w), 1)
    n_groups = w // HEAD_DIM
    t = x.shape[0]
    for ch in range(t // SGU_CHUNK):
        rows = slice(ch * SGU_CHUNK, (ch + 1) * SGU_CHUNK)
        av = a_vb[rows]
        mix = jnp.zeros((SGU_CHUNK, w), F32)
        for g in range(n_groups):
            part = _dot(sgw_ref[g], av)
            mix = jnp.where(lane_w // HEAD_DIM == g, part, mix)
        oa_ref[rows, :] = (a_u[rows] * (mix + sgb_ref[...])).astype(BF16)

    o = 2 * w
    d_q = proj[:, o:o + 256]
    d_k = proj[:, o + 256:o + 512]
    d_v = proj[:, o + 512:o + 768]
    g_q = proj[:, o + 768:o + 1792]
    g_k = proj[:, o + 1792:o + 1920]
    g_v = proj[:, o + 1920:o + 2048]

    lane = lax.broadcasted_iota(jnp.int32, (1, LANES), 1)
    low = lane < HEAD_DIM
    k2 = g_k * g_k
    s_low = jnp.sum(jnp.where(low, k2, 0.0), axis=-1, keepdims=True)
    s_all = jnp.sum(k2, axis=-1, keepdims=True)
    ms = jnp.where(low, s_low, s_all - s_low) * (1.0 / HEAD_DIM)
    g_k = g_k * lax.rsqrt(ms + EPS) * kng_ref[...]

    if emit_kv:
        kd32_ref[...] = d_k
        vd32_ref[...] = d_v
        kg32_ref[...] = g_k
        vg32_ref[...] = g_v

    if rope:
        g_k = _rope(g_k, cg_ref[...], sg_ref[...], HEAD_DIM // 2)
    gk_ref[...] = g_k.astype(BF16)
    gv_ref[...] = g_v.astype(BF16)
    dv_ref[...] = d_v.astype(BF16)

    for s in range(2):
        cols = slice(s * LANES, (s + 1) * LANES)
        q = d_q[:, cols]
        k = d_k[:, cols]
        if rope:
            q = _rope(q, cd_ref[...], sd_ref[...], DIFF_DQK // 2)
            k = _rope(k, cd_ref[...], sd_ref[...], DIFF_DQK // 2)
        dq_ref[:, cols] = (q * DIFF_SCALE).astype(BF16)
        dk_ref[:, cols] = k.astype(BF16)

    for s in range(g_q.shape[1] // LANES):
        cols = slice(s * LANES, (s + 1) * LANES)
        q = g_q[:, cols]
        ms = jnp.sum(q * q, axis=-1, keepdims=True) * (1.0 / HEAD_DIM)
        q = q * lax.rsqrt(ms + EPS) * qng_ref[...]
        if rope:
            q = _rope(q, cg_ref[...], sg_ref[...], HEAD_DIM // 2)
        gq_ref[:, cols] = (q * GQA_SCALE).astype(BF16)


def _front(x, mod_l, row_of_block, g1, win, sgn, sgw, sgb, qng, kng, rope_tabs, emit_kv):
    n, d = x.shape
    t = FRONT_ROWS
    nb = n // t
    rope = rope_tabs is not None
    const2 = lambda i: (0, 0)
    in_specs = [
        pl.BlockSpec((t, d), lambda i: (i, 0)),
        pl.BlockSpec((1, ADA_CHUNKS, d), lambda i: (row_of_block(i), 0, 0)),
        pl.BlockSpec((1, d), const2),
        pl.BlockSpec(win.shape, const2),
        pl.BlockSpec(sgn.shape, const2),
        pl.BlockSpec(sgw.shape, lambda i: (0, 0, 0)),
        pl.BlockSpec(sgb.shape, const2),
        pl.BlockSpec(qng.shape, const2),
        pl.BlockSpec(kng.shape, const2),
    ]
    args = [x, mod_l, g1, win, sgn, sgw, sgb, qng, kng]
    if rope:
        blocks_per_seq = rope_tabs[0].shape[0] // t
        for tab in rope_tabs:
            in_specs.append(pl.BlockSpec((t, LANES), lambda i: (i % blocks_per_seq, 0)))
            args.append(tab)
    widths = [(256, BF16), (256, BF16), (256, BF16), (256, BF16), (1024, BF16), (128, BF16), (128, BF16)]
    if emit_kv:
        widths += [(256, F32), (256, F32), (128, F32), (128, F32)]
    out_specs = [pl.BlockSpec((t, wd), lambda i: (i, 0)) for wd, _ in widths]
    out_shape = [jax.ShapeDtypeStruct((n, wd), dt) for wd, dt in widths]
    return pl.pallas_call(
        functools.partial(_front_kernel, rope=rope, emit_kv=emit_kv, d_model=d, sgu_w_cols=sgn.shape[1]),
        grid=(nb,),
        in_specs=in_specs,
        out_specs=out_specs,
        out_shape=out_shape,
        compiler_params=_params(1),
        name="front",
    )(*args)


def _softmax_pv(s, v):
    mx = jnp.max(s, axis=-1, keepdims=True)
    e = jnp.exp(s - mx)
    z = jnp.sum(e, axis=-1, keepdims=True)
    return _dot(e.astype(BF16), v) * (1.0 / z)


def _attn_kernel(x_ref, mod_ref, oa_ref, dq_ref, gq_ref, dk_ref, dv_ref, gk_ref, gv_ref,
                 lam_ref, subln_ref, woab_ref, woc_ref, out_ref, oc_ref, *, lam_init):
    lam = (jnp.exp(jnp.sum(lam_ref[0:1] * lam_ref[1:2], axis=-1, keepdims=True))
           - jnp.exp(jnp.sum(lam_ref[2:3] * lam_ref[3:4], axis=-1, keepdims=True)) + lam_init)

    dq = dq_ref[...]
    dk = dk_ref[...]
    dv = dv_ref[...]
    w = dq.shape[1]
    lane = lax.broadcasted_iota(jnp.int32, (1, w), 1)
    out_b = jnp.zeros(dq.shape, F32)
    inv = jnp.zeros(dq.shape, F32)
    n_heads = w // HEAD_DIM
    for hd in range(n_heads):
        parts = []
        for xx in range(2):
            sel = (lane // DIFF_DQK) == (2 * hd + xx)
            qm = jnp.where(sel, dq, jnp.zeros_like(dq))
            parts.append(_softmax_pv(_dot_nt(qm, dk), dv))
        o = parts[0] - lam * parts[1]
        in_head = (lane // HEAD_DIM) == hd
        ms = jnp.sum(jnp.where(in_head, o * o, 0.0), axis=-1, keepdims=True) * (1.0 / HEAD_DIM)
        out_b = jnp.where(in_head, o, out_b)
        inv = jnp.where(in_head, lax.rsqrt(ms + EPS), inv)
    out_b = out_b * inv * subln_ref[...] * (1.0 - lam_init)

    gk = gk_ref[...]
    gv = gv_ref[...]
    for hq in range(gq_ref.shape[1] // LANES):
        cols = slice(hq * LANES, (hq + 1) * LANES)
        oc_ref[:, cols] = _softmax_pv(_dot_nt(gq_ref[:, cols], gk), gv).astype(BF16)

    ab = jnp.concatenate([oa_ref[...], out_b.astype(BF16)], axis=-1)
    mix = _dot(ab, woab_ref[...]) + _dot(oc_ref[...], woc_ref[...])
    out_ref[...] = x_ref[...] + mod_ref[0][2:3] * mix


def _attn(x3, mod_l, row_of_batch, oa, dq, gq, dk, dv, gk, gv, lam_vecs, subln, woab, woc, lam_init):
    b, l, d = x3.shape
    tq = min(ATTN_ROWS, l)
    lk = dk.shape[1]
    qspec = lambda wd: pl.BlockSpec((None, tq, wd), lambda i, j: (i, j, 0))
    kspec = lambda wd: pl.BlockSpec((None, lk, wd), lambda i, j: (i, 0, 0))
    const2 = lambda i, j: (0, 0)
    return pl.pallas_call(
        functools.partial(_attn_kernel, lam_init=lam_init),
        grid=(b, l // tq),
        in_specs=[
            qspec(d),
            pl.BlockSpec((1, ADA_CHUNKS, d), lambda i, j: (row_of_batch(i), 0, 0)),
            qspec(oa.shape[2]), qspec(dq.shape[2]), qspec(gq.shape[2]),
            kspec(dk.shape[2]), kspec(dv.shape[2]), kspec(gk.shape[2]), kspec(gv.shape[2]),
            pl.BlockSpec(lam_vecs.shape, const2),
            pl.BlockSpec(subln.shape, const2),
            pl.BlockSpec(woab.shape, const2),
            pl.BlockSpec(woc.shape, const2),
        ],
        out_specs=qspec(d),
        out_shape=jax.ShapeDtypeStruct((b, l, d), F32),
        scratch_shapes=[pltpu.VMEM((tq, gq.shape[2]), BF16)],
        compiler_params=_params(2),
        name="attn",
    )(x3, mod_l, oa, dq, gq, dk, dv, gk, gv, lam_vecs, subln, woab, woc)


def _sorting_network(n):
    pairs = []

    def merge(lo, hi, r):
        step = r * 2
        if step < hi - lo:
            merge(lo, hi, step)
            merge(lo + r, hi, step)
            pairs.extend((i, i + r) for i in range(lo + r, hi - r, step))
        else:
            pairs.append((lo, lo + r))

    def sort(lo, hi):
        if hi - lo >= 1:
            mid = lo + (hi - lo) // 2
            sort(lo, mid)
            sort(mid + 1, hi)
            merge(lo, hi, 1)

    sort(0, n - 1)
    return pairs


def _extract_top(levels, k, rows_ref=None):
    levels = list(levels)
    out = []
    for r in range(k):
        head = levels[0]
        mx = jnp.max(head, axis=0, keepdims=True)
        out.append(mx)
        if rows_ref is not None:
            rows_ref[r:r + 1, :] = mx
        need = k - r - 1
        if need:
            hit = head == mx
            for m in range(min(need, len(levels))):
                below = levels[m + 1] if m + 1 < len(levels) else -jnp.inf
                levels[m] = jnp.where(hit, below, levels[m])
    return out


def _top_rows(s, k, rows_ref):
    blocks = [s[v * 8:(v + 1) * 8] for v in range(s.shape[0] // 8)]
    for i, j in _sorting_network(len(blocks)):
        blocks[i], blocks[j] = jnp.maximum(blocks[i], blocks[j]), jnp.minimum(blocks[i], blocks[j])
    _extract_top(blocks, k, rows_ref)


def _count_true(pred, rows):
    c8 = pred(rows[7])
    c4 = pred(jnp.where(c8, rows[11], rows[3]))
    c2 = pred(jnp.where(c8, jnp.where(c4, rows[13], rows[9]), jnp.where(c4, rows[5], rows[1])))
    hi = jnp.where(c4, jnp.where(c2, rows[14], rows[12]), jnp.where(c2, rows[10], rows[8]))
    lo = jnp.where(c4, jnp.where(c2, rows[6], rows[4]), jnp.where(c2, rows[2], rows[0]))
    c1 = pred(jnp.where(c8, hi, lo))
    c16 = pred(rows[15])
    one = lambda c, w: jnp.where(c, float(w), 0.0)
    return (one(c8, 8) + one(c4, 4)) + (one(c2, 2) + one(c1, 1)) + one(c16, 1)


def _pair_words(x):
    u = pltpu.bitcast(x.astype(BF16).astype(F32), jnp.uint32)
    return u | lax.shift_right_logical(u, jnp.uint32(16))


def _route_head(hd, s1, s2, v1_ref, v2_ref, r2_ref, bb_ref, nn_ref, aa_ref, cols):
    _top_rows(s1, PEER_TOPK, v1_ref)
    _top_rows(s2, PEER_TOPK, v2_ref)
    v1 = v1_ref[...]
    v2 = v2_ref[...]
    v2_rows = [v2[k:k + 1] for k in range(PEER_TOPK)]
    k1 = lax.broadcasted_iota(jnp.int32, v1.shape, 0)
    levels = [jnp.where((k1 + 1) * (k2 + 1) <= PEER_TOPK, v1 + v2_rows[k2], -jnp.inf)
              for k2 in range(PEER_TOPK)]
    best = _extract_top(levels, PEER_TOPK)
    tau = best[-1]
    z = jnp.ones_like(tau)
    for r in range(1, PEER_TOPK):
        z = z + jnp.exp(best[r] - best[0])
    n = _count_true(lambda row: s1 + row >= tau, v2_rows)
    rank2 = _count_true(lambda row: row > s2, v2_rows)
    a = jnp.exp(s1 - v1[0:1]) * (1.0 / z)
    r2_ref[hd, :, cols] = rank2.astype(BF16)
    bb_ref[hd, :, cols] = jnp.exp(s2 - v2[0:1]).astype(BF16)
    n_words = _pair_words(n)
    a_words = _pair_words(a)
    for k in range(N_KEYS // PEER_KEY_ROWS):
        nn_ref[hd, k, :, cols] = n_words[k * PEER_KEY_ROWS:(k + 1) * PEER_KEY_ROWS]
        aa_ref[hd, k, :, cols] = a_words[k * PEER_KEY_ROWS:(k + 1) * PEER_KEY_ROWS]


def _peer_kernel(x_ref, mod_ref, g2_ref, wqt_ref, sk_ref, u_ref, vt_ref, fin_ref, out_ref,
                 hb_ref, r2_ref, bb_ref, nn_ref, aa_ref, v1_ref, v2_ref, pre_ref, wt_ref, acc_ref,
                 *, final_norm):
    c = pl.program_id(1)
    t = x_ref.shape[0]

    @pl.when(c == 0)
    def _routing():
        m = mod_ref[0]
        h = _rms(x_ref[...]) * g2_ref[...] * (1.0 + m[4:5]) + m[3:4]
        hb = h.astype(BF16)
        hb_ref[...] = hb
        qt = _dot_nt(wqt_ref[...], hb)
        for hd in range(PEER_HEADS):
            base = hd * 2 * PEER_HALF
            s1 = _dot(sk_ref[0], qt[base:base + PEER_HALF].astype(BF16))
            s2 = _dot(sk_ref[1], qt[base + PEER_HALF:base + 2 * PEER_HALF].astype(BF16))
            for lt in range(t // LANES):
                cols = slice(lt * LANES, (lt + 1) * LANES)
                _route_head(hd, s1[:, cols], s2[:, cols], v1_ref, v2_ref, r2_ref, bb_ref, nn_ref, aa_ref, cols)
        acc_ref[...] = jnp.zeros(acc_ref.shape, F32)

    pre_ref[...] = _dot_nt(u_ref[...], hb_ref[...])

    sub = BF16_ROWS
    for r in range(PEER_KEY_ROWS):
        for lt in range(t // LANES):
            cols = slice(lt * LANES, (lt + 1) * LANES)
            n_b, a_b = [], []
            for hd in range(PEER_HEADS):
                n_row = jnp.broadcast_to(nn_ref[hd, c, r:r + 1, cols], (sub // 2, LANES))
                a_row = jnp.broadcast_to(aa_ref[hd, c, r:r + 1, cols], (sub // 2, LANES))
                n_b.append(pltpu.bitcast(n_row, BF16))
                a_b.append(pltpu.bitcast(a_row, BF16))
            for jb in range(N_KEYS // sub):
                keys = slice(jb * sub, (jb + 1) * sub)
                g = jnp.zeros((sub, LANES), BF16)
                for hd in range(PEER_HEADS):
                    kept = jnp.where(r2_ref[hd, keys, cols] < n_b[hd], bb_ref[hd, keys, cols],
                                     jnp.zeros((sub, LANES), BF16))
                    g = g + kept * a_b[hd]
                rows = slice(r * N_KEYS + jb * sub, r * N_KEYS + (jb + 1) * sub)
                wt_ref[rows, cols] = _gelu_folded(pre_ref[rows, cols].astype(BF16)) * g
    acc_ref[...] += _dot(vt_ref[...], wt_ref[...])

    @pl.when(c == pl.num_programs(1) - 1)
    def _finish():
        y = x_ref[...] + mod_ref[0][5:6] * acc_ref[...].T
        if final_norm:
            y = _rms(y) * fin_ref[...]
        out_ref[...] = y


def _peer(x, mod_l, row_of_block, g2, wqt, sk, u, vt, fin, final_norm):
    n, d = x.shape
    t = PEER_ROWS
    ec = PEER_KEY_ROWS * N_KEYS
    nc = u.shape[0] // ec
    const2 = lambda i, c: (0, 0)
    gate = pltpu.VMEM((PEER_HEADS, N_KEYS, t), BF16)
    gate_rows = pltpu.VMEM((PEER_HEADS, N_KEYS // PEER_KEY_ROWS, PEER_KEY_ROWS, t), jnp.uint32)
    return pl.pallas_call(
        functools.partial(_peer_kernel, final_norm=final_norm),
        grid=(n // t, nc),
        in_specs=[
            pl.BlockSpec((t, d), lambda i, c: (i, 0)),
            pl.BlockSpec((1, ADA_CHUNKS, d), lambda i, c: (row_of_block(i), 0, 0)),
            pl.BlockSpec((1, d), const2),
            pl.BlockSpec(wqt.shape, const2),
            pl.BlockSpec(sk.shape, lambda i, c: (0, 0, 0)),
            pl.BlockSpec((ec, d), lambda i, c: (c, 0)),
            pl.BlockSpec((None, d, ec), lambda i, c: (c, 0, 0)),
            pl.BlockSpec((1, d), const2),
        ],
        out_specs=pl.BlockSpec((t, d), lambda i, c: (i, 0)),
        out_shape=jax.ShapeDtypeStruct((n, d), F32),
        scratch_shapes=[
            pltpu.VMEM((t, d), BF16),
            gate, gate, gate_rows, gate_rows,
            pltpu.VMEM((PEER_TOPK, LANES), F32),
            pltpu.VMEM((PEER_TOPK, LANES), F32),
            pltpu.VMEM((ec, t), F32),
            pltpu.VMEM((ec, t), BF16),
            pltpu.VMEM((d, t), F32),
        ],
        compiler_params=_params(2),
        name="peer",
    )(x, mod_l, g2, wqt, sk, u, vt, fin)


def _rope_tables(n_tokens, dim):
    rows = n_tokens // GRID_W
    row = jnp.repeat(jnp.arange(rows, dtype=F32), GRID_W)
    col = jnp.tile(jnp.arange(GRID_W, dtype=F32), rows)
    quarter = dim // 4
    inv = ROPE_THETA ** (-jnp.arange(quarter, dtype=F32) / quarter)
    ang = jnp.concatenate([row[:, None] * inv, col[:, None] * inv], axis=-1)
    cos, sin = jnp.cos(ang), jnp.sin(ang)
    reps = LANES // dim
    return (jnp.tile(jnp.concatenate([cos, cos], axis=-1), (1, reps)),
            jnp.tile(jnp.concatenate([-sin, sin], axis=-1), (1, reps)))


def _slot_pad(a, axis, n_heads):
    parts = []
    for hq in range(n_heads):
        blk = lax.slice_in_dim(a, hq * HEAD_DIM, (hq + 1) * HEAD_DIM, axis=axis)
        zero = jnp.zeros_like(blk)
        parts += [blk, zero] if (hq // GQA_GROUP) == 0 else [zero, blk]
    return jnp.concatenate(parts, axis=axis)


def kernel(x_prompt, x_sample, cache_diff_k, cache_diff_v, cache_gqa_k, cache_gqa_v, c, c_ctx, ada_w, ada_b, norm1_g, norm2_g, w_in, sgu_norm_g, sgu_w, sgu_b, diff_lq1, diff_lk1, diff_lq2, diff_lk2, diff_subln_g, gqa_qnorm_g, gqa_knorm_g, w_out, peer_wq, peer_subkeys, peer_u, peer_v, final_g):
    depth, d, _ = w_in.shape
    bp, lp, _ = x_prompt.shape
    bs, ls, _ = x_sample.shape
    past = cache_diff_k.shape[2]
    sgu_cols = sgu_norm_g.shape[1]
    diff_cols = diff_subln_g.shape[1] * cache_diff_v.shape[3]
    kv_cols = cache_gqa_k.shape[3] * HEAD_DIM
    gq_cols = kv_cols * GQA_GROUP
    n_gqa_heads = gq_cols // HEAD_DIM
    assert kv_cols == LANES and diff_cols == 2 * LANES and sgu_cols == 2 * LANES

    n_rows = -(-(1 + bs) // 8) * 8
    cvecs = jnp.zeros((n_rows, d), F32).at[0].set(c_ctx).at[1:1 + bs].set(c)
    mod = _modulation(cvecs, ada_w, ada_b).reshape(depth, n_rows, ADA_CHUNKS, d)

    o = 2 * sgu_cols + 3 * diff_cols
    win_ext = jnp.concatenate(
        [w_in[:, :, :o], _slot_pad(w_in[:, :, o:o + gq_cols], 2, n_gqa_heads), w_in[:, :, o + gq_cols:]],
        axis=2).astype(BF16)
    woab = w_out[:, :sgu_cols + diff_cols].astype(BF16)
    woc = _slot_pad(w_out[:, sgu_cols + diff_cols:], 1, n_gqa_heads).astype(BF16)
    sgw = sgu_w.astype(BF16)
    sgb = jnp.repeat(jnp.swapaxes(sgu_b, 1, 2), HEAD_DIM, axis=2)
    qng = jnp.tile(gqa_qnorm_g, (1, 2))[:, None, :]
    kng = jnp.tile(gqa_knorm_g, (1, 2))[:, None, :]
    subln = jnp.tile(diff_subln_g, (1, diff_cols // HEAD_DIM))[:, None, :]
    lam_vecs = jnp.stack([diff_lq1, diff_lk1, diff_lq2, diff_lk2], axis=1)
    wqt = jnp.swapaxes(peer_wq, 1, 2).astype(BF16)
    sk = peer_subkeys.astype(BF16)
    u_tab = peer_u.astype(BF16)
    ec = PEER_KEY_ROWS * N_KEYS
    vt_tab = jnp.swapaxes(peer_v.astype(BF16).reshape(depth, -1, ec, d), 2, 3)
    tabs = _rope_tables(ls, DIFF_DQK) + _rope_tables(ls, HEAD_DIM)

    def run(x3, sample):
        b, l, _ = x3.shape
        x = x3.reshape(b * l, d)
        if sample:
            row_front = lambda i: 1 + i // (l // FRONT_ROWS)
            row_attn = lambda i: 1 + i
            row_peer = lambda i: 1 + i // (l // PEER_ROWS)
        else:
            row_front = row_attn = row_peer = lambda i: 0
        kv_out = []
        for layer in range(depth):
            lam_init = 0.8 - 0.6 * math.exp(-0.3 * layer)
            mod_l = mod[layer]
            outs = _front(x, mod_l, row_front, norm1_g[layer][None], win_ext[layer], sgu_norm_g[layer][None],
                          sgw[layer], sgb[layer], qng[layer], kng[layer], tabs if sample else None, not sample)
            oa, dq, dk, dv, gq, gk, gv = [a.reshape(b, l, a.shape[1]) for a in outs[:7]]
            if sample:
                cat = lambda new, old: jnp.concatenate(
                    [new, old[:, layer].reshape(b, past, -1).astype(BF16)], axis=1)
                dk, dv = cat(dk, cache_diff_k), cat(dv, cache_diff_v)
                gk, gv = cat(gk, cache_gqa_k), cat(gv, cache_gqa_v)
            else:
                kv_out.append(outs[7:])
            x3n = _attn(x.reshape(b, l, d), mod_l, row_attn, oa, dq, gq, dk, dv, gk, gv,
                        lam_vecs[layer], subln[layer], woab[layer], woc[layer], lam_init)
            x = _peer(x3n.reshape(b * l, d), mod_l, row_peer, norm2_g[layer][None], wqt[layer], sk[layer],
                      u_tab[layer], vt_tab[layer], final_g[None], layer == depth - 1)
        return x.reshape(b, l, d), kv_out

    y_prompt, kv = run(x_prompt, False)
    y_sample, _ = run(x_sample, True)
    stack = lambda idx, shape: jnp.stack([kv[layer][idx].reshape(shape) for layer in range(depth)], axis=1)
    new_diff_k = stack(0, (bp, lp) + cache_diff_k.shape[3:])
    new_diff_v = stack(1, (bp, lp) + cache_diff_v.shape[3:])
    new_gqa_k = stack(2, (bp, lp) + cache_gqa_k.shape[3:])
    new_gqa_v = stack(3, (bp, lp) + cache_gqa_v.shape[3:])
    return (y_prompt, y_sample, new_diff_k, new_diff_v, new_gqa_k, new_gqa_v)
```

```python
import functools
import math

import jax
import jax.numpy as jnp
from jax import lax
from jax.experimental import pallas as pl
from jax.experimental.pallas import tpu as pltpu

F32 = jnp.float32
BF16 = jnp.bfloat16

HEAD_DIM = 64
SGU_CHUNK = 128
DIFF_DQK = 32
GQA_GROUP = 4
ADA_CHUNKS = 6
N_KEYS = 128
PEER_HEADS = 8
PEER_TOPK = 16
PEER_HALF = 64
GRID_W = 64
ROPE_THETA = 10000.0
EPS = 1e-6
DIFF_SCALE = DIFF_DQK ** -0.5
GQA_SCALE = HEAD_DIM ** -0.5

LANES = 128
BF16_ROWS = 16
VMEM_LIMIT_BYTES = 56 * 1024 * 1024

FRONT_ROWS = 256
ATTN_ROWS = 256
PEER_ROWS = 512
PEER_KEY_ROWS = 16


def _dot(a, b):
    return jnp.dot(a, b, preferred_element_type=F32)


def _dot_nt(a, b):
    return lax.dot_general(a, b, (((1,), (1,)), ((), ())), preferred_element_type=F32)


def _rms(x):
    return x * lax.rsqrt(jnp.mean(x * x, axis=-1, keepdims=True) + EPS)


def _gelu(x):
    c = math.sqrt(2.0 / math.pi)
    return 0.5 * x * (1.0 + jnp.tanh(c * (x + 0.044715 * (x * x * x))))


def _gelu_folded(x):
    c = math.sqrt(2.0 / math.pi)
    hx = 0.5 * x
    return hx + hx * jnp.tanh(x * (c + (0.044715 * c) * (x * x)))


def _params(n_axes, flags=None):
    return pltpu.CompilerParams(
        dimension_semantics=("arbitrary",) * n_axes, vmem_limit_bytes=VMEM_LIMIT_BYTES, flags=flags)


def _mod_kernel(c_ref, w_ref, b_ref, o_ref):
    c = c_ref[...]
    s = c / (1.0 + jnp.exp(-c))
    o_ref[0] = _dot(s.astype(BF16), w_ref[0].astype(BF16)) + b_ref[0]


def _modulation(cvecs, ada_w, ada_b):
    depth, d, n = ada_w.shape
    rows = cvecs.shape[0]
    tn = n // 4
    return pl.pallas_call(
        _mod_kernel,
        grid=(depth, n // tn),
        in_specs=[
            pl.BlockSpec((rows, d), lambda l, j: (0, 0)),
            pl.BlockSpec((1, d, tn), lambda l, j: (l, 0, j)),
            pl.BlockSpec((1, 1, tn), lambda l, j: (l, 0, j)),
        ],
        out_specs=pl.BlockSpec((1, rows, tn), lambda l, j: (l, 0, j)),
        out_shape=jax.ShapeDtypeStruct((depth, rows, n), F32),
        compiler_params=_params(2),
        name="modulation",
    )(cvecs, ada_w, ada_b.reshape(depth, 1, n))


def _rope(x, cos, sin_signed, half):
    lane = lax.broadcasted_iota(jnp.int32, (1, LANES), 1)
    first = (lane % (2 * half)) < half
    partner = jnp.where(first, pltpu.roll(x, LANES - half, 1), pltpu.roll(x, half, 1))
    return x * cos + partner * sin_signed


def _front_kernel(*refs, rope, emit_kv, d_model, sgu_w_cols):
    (x_ref, mod_ref, g1_ref, win_ref, sgn_ref, sgw_ref, sgb_ref, qng_ref, kng_ref) = refs[:9]
    pos = 9
    if rope:
        cd_ref, sd_ref, cg_ref, sg_ref = refs[pos:pos + 4]
        pos += 4
    oa_ref, dq_ref, dk_ref, dv_ref, gq_ref, gk_ref, gv_ref = refs[pos:pos + 7]
    pos += 7
    if emit_kv:
        kd32_ref, vd32_ref, kg32_ref, vg32_ref = refs[pos:pos + 4]

    x = x_ref[...]
    m = mod_ref[0]
    h = _rms(x) * g1_ref[...] * (1.0 + m[1:2]) + m[0:1]
    proj = _dot(h.astype(BF16), win_ref[...])

    w = sgu_w_cols
    a_u = _gelu(proj[:, 0:w])
    a_v = _rms(_gelu(proj[:, w:2 * w])) * sgn_ref[...]
    a_vb = a_v.astype(BF16)
    lane_w = lax.broadcasted_iota(jnp.int32, (1, w), 1)
    n_groups = w // HEAD_DIM
    t = x.shape[0]
    for ch in range(t // SGU_CHUNK):
        rows = slice(ch * SGU_CHUNK, (ch + 1) * SGU_CHUNK)
        av = a_vb[rows]
        mix = jnp.zeros((SGU_CHUNK, w), F32)
        for g in range(n_groups):
            part = _dot(sgw_ref[g], av)
            mix = jnp.where(lane_w // HEAD_DIM == g, part, mix)
        oa_ref[rows, :] = (a_u[rows] * (mix + sgb_ref[...])).astype(BF16)

    o = 2 * w
    d_q = proj[:, o:o + 256]
    d_k = proj[:, o + 256:o + 512]
    d_v = proj[:, o + 512:o + 768]
    g_q = proj[:, o + 768:o + 1792]
    g_k = proj[:, o + 1792:o + 1920]
    g_v = proj[:, o + 1920:o + 2048]

    lane = lax.broadcasted_iota(jnp.int32, (1, LANES), 1)
    low = lane < HEAD_DIM
    k2 = g_k * g_k
    s_low = jnp.sum(jnp.where(low, k2, 0.0), axis=-1, keepdims=True)
    s_all = jnp.sum(k2, axis=-1, keepdims=True)
    ms = jnp.where(low, s_low, s_all - s_low) * (1.0 / HEAD_DIM)
    g_k = g_k * lax.rsqrt(ms + EPS) * kng_ref[...]

    if emit_kv:
        kd32_ref[...] = d_k
        vd32_ref[...] = d_v
        kg32_ref[...] = g_k
        vg32_ref[...] = g_v

    if rope:
        g_k = _rope(g_k, cg_ref[...], sg_ref[...], HEAD_DIM // 2)
    gk_ref[...] = g_k.astype(BF16)
    ones = jnp.ones((x.shape[0], LANES), BF16)
    gv_ref[...] = jnp.concatenate([g_v.astype(BF16), ones], axis=-1)
    dv_ref[...] = jnp.concatenate(
        [d_v[:, :LANES].astype(BF16), ones, d_v[:, LANES:].astype(BF16), ones], axis=-1)

    for s in range(2):
        cols = slice(s * LANES, (s + 1) * LANES)
        q = d_q[:, cols]
        k = d_k[:, cols]
        if rope:
            q = _rope(q, cd_ref[...], sd_ref[...], DIFF_DQK // 2)
            k = _rope(k, cd_ref[...], sd_ref[...], DIFF_DQK // 2)
        dq_ref[:, cols] = (q * DIFF_SCALE).astype(BF16)
        dk_ref[:, cols] = k.astype(BF16)

    for s in range(g_q.shape[1] // LANES):
        cols = slice(s * LANES, (s + 1) * LANES)
        q = g_q[:, cols]
        ms = jnp.sum(q * q, axis=-1, keepdims=True) * (1.0 / HEAD_DIM)
        q = q * lax.rsqrt(ms + EPS) * qng_ref[...]
        if rope:
            q = _rope(q, cg_ref[...], sg_ref[...], HEAD_DIM // 2)
        gq_ref[:, cols] = (q * GQA_SCALE).astype(BF16)


def _front(x, mod_l, row_of_block, g1, win, sgn, sgw, sgb, qng, kng, rope_tabs, emit_kv):
    n, d = x.shape
    t = FRONT_ROWS
    nb = n // t
    rope = rope_tabs is not None
    const2 = lambda i: (0, 0)
    in_specs = [
        pl.BlockSpec((t, d), lambda i: (i, 0)),
        pl.BlockSpec((1, ADA_CHUNKS, d), lambda i: (row_of_block(i), 0, 0)),
        pl.BlockSpec((1, d), const2),
        pl.BlockSpec(win.shape, const2),
        pl.BlockSpec(sgn.shape, const2),
        pl.BlockSpec(sgw.shape, lambda i: (0, 0, 0)),
        pl.BlockSpec(sgb.shape, const2),
        pl.BlockSpec(qng.shape, const2),
        pl.BlockSpec(kng.shape, const2),
    ]
    args = [x, mod_l, g1, win, sgn, sgw, sgb, qng, kng]
    if rope:
        blocks_per_seq = rope_tabs[0].shape[0] // t
        for tab in rope_tabs:
            in_specs.append(pl.BlockSpec((t, LANES), lambda i: (i % blocks_per_seq, 0)))
            args.append(tab)
    widths = [(256, BF16), (256, BF16), (256, BF16), (512, BF16), (1024, BF16), (128, BF16), (256, BF16)]
    if emit_kv:
        widths += [(256, F32), (256, F32), (128, F32), (128, F32)]
    out_specs = [pl.BlockSpec((t, wd), lambda i: (i, 0)) for wd, _ in widths]
    out_shape = [jax.ShapeDtypeStruct((n, wd), dt) for wd, dt in widths]
    return pl.pallas_call(
        functools.partial(_front_kernel, rope=rope, emit_kv=emit_kv, d_model=d, sgu_w_cols=sgn.shape[1]),
        grid=(nb,),
        in_specs=in_specs,
        out_specs=out_specs,
        out_shape=out_shape,
        compiler_params=_params(1),
        name="front",
    )(*args)


def _softmax_pv(s, v_ones):
    sb = s.astype(BF16)
    e = jnp.exp(sb - jnp.max(sb, axis=-1, keepdims=True))
    pv = _dot(e, v_ones)
    return pv[:, :LANES] * (1.0 / pv[:, LANES:])


def _attn_kernel(x_ref, mod_ref, oa_ref, dq_ref, gq_ref, dk_ref, dv_ref, gk_ref, gv_ref,
                 lam_ref, subln_ref, woab_ref, woc_ref, out_ref, oc_ref, *, lam_init):
    lam = (jnp.exp(jnp.sum(lam_ref[0:1] * lam_ref[1:2], axis=-1, keepdims=True))
           - jnp.exp(jnp.sum(lam_ref[2:3] * lam_ref[3:4], axis=-1, keepdims=True)) + lam_init)

    dq = dq_ref[...]
    dk = dk_ref[...]
    w = dq.shape[1]
    lane = lax.broadcasted_iota(jnp.int32, (1, w), 1)
    lane_pair = lax.broadcasted_iota(jnp.int32, (1, LANES), 1)
    heads_per_pair = LANES // HEAD_DIM
    pairs = []
    for pr in range(w // LANES):
        dv = dv_ref[:, pr * 2 * LANES:(pr + 1) * 2 * LANES]
        slot = jnp.zeros((dq.shape[0], LANES), F32)
        inv = jnp.zeros((dq.shape[0], LANES), F32)
        for sub in range(heads_per_pair):
            hd = pr * heads_per_pair + sub
            parts = []
            for xx in range(2):
                sel = (lane // DIFF_DQK) == (2 * hd + xx)
                qm = jnp.where(sel, dq, jnp.zeros_like(dq))
                parts.append(_softmax_pv(_dot_nt(qm, dk), dv))
            o = parts[0] - lam * parts[1]
            in_head = (lane_pair // HEAD_DIM) == sub
            ms = jnp.sum(jnp.where(in_head, o * o, 0.0), axis=-1, keepdims=True) * (1.0 / HEAD_DIM)
            slot = jnp.where(in_head, o, slot)
            inv = jnp.where(in_head, lax.rsqrt(ms + EPS), inv)
        pairs.append(slot * inv)
    out_b = jnp.concatenate(pairs, axis=-1) * subln_ref[...] * (1.0 - lam_init)

    gk = gk_ref[...]
    gv = gv_ref[...]
    for hq in range(gq_ref.shape[1] // LANES):
        cols = slice(hq * LANES, (hq + 1) * LANES)
        oc_ref[:, cols] = _softmax_pv(_dot_nt(gq_ref[:, cols], gk), gv).astype(BF16)

    ab = jnp.concatenate([oa_ref[...], out_b.astype(BF16)], axis=-1)
    mix = _dot(ab, woab_ref[...]) + _dot(oc_ref[...], woc_ref[...])
    out_ref[...] = x_ref[...] + mod_ref[0][2:3] * mix


def _attn(x3, mod_l, row_of_batch, oa, dq, gq, dk, dv, gk, gv, lam_vecs, subln, woab, woc, lam_init):
    b, l, d = x3.shape
    tq = min(ATTN_ROWS, l)
    lk = dk.shape[1]
    qspec = lambda wd: pl.BlockSpec((None, tq, wd), lambda i, j: (i, j, 0))
    kspec = lambda wd: pl.BlockSpec((None, lk, wd), lambda i, j: (i, 0, 0))
    const2 = lambda i, j: (0, 0)
    return pl.pallas_call(
        functools.partial(_attn_kernel, lam_init=lam_init),
        grid=(b, l // tq),
        in_specs=[
            qspec(d),
            pl.BlockSpec((1, ADA_CHUNKS, d), lambda i, j: (row_of_batch(i), 0, 0)),
            qspec(oa.shape[2]), qspec(dq.shape[2]), qspec(gq.shape[2]),
            kspec(dk.shape[2]), kspec(dv.shape[2]), kspec(gk.shape[2]), kspec(gv.shape[2]),
            pl.BlockSpec(lam_vecs.shape, const2),
            pl.BlockSpec(subln.shape, const2),
            pl.BlockSpec(woab.shape, const2),
            pl.BlockSpec(woc.shape, const2),
        ],
        out_specs=qspec(d),
        out_shape=jax.ShapeDtypeStruct((b, l, d), F32),
        scratch_shapes=[pltpu.VMEM((tq, gq.shape[2]), BF16)],
        compiler_params=_params(2),
        name="attn",
    )(x3, mod_l, oa, dq, gq, dk, dv, gk, gv, lam_vecs, subln, woab, woc)


def _sorting_network(n):
    pairs = []

    def merge(lo, hi, r):
        step = r * 2
        if step < hi - lo:
            merge(lo, hi, step)
            merge(lo + r, hi, step)
            pairs.extend((i, i + r) for i in range(lo + r, hi - r, step))
        else:
            pairs.append((lo, lo + r))

    def sort(lo, hi):
        if hi - lo >= 1:
            mid = lo + (hi - lo) // 2
            sort(lo, mid)
            sort(mid + 1, hi)
            merge(lo, hi, 1)

    sort(0, n - 1)
    return pairs


def _extract_top(levels, k, rows_ref=None):
    levels = list(levels)
    out = []
    for r in range(k):
        head = levels[0]
        mx = jnp.max(head, axis=0, keepdims=True)
        out.append(mx)
        if rows_ref is not None:
            rows_ref[r:r + 1, :] = mx
        need = k - r - 1
        if need:
            hit = head == mx
            for m in range(min(need, len(levels))):
                below = levels[m + 1] if m + 1 < len(levels) else -jnp.inf
                levels[m] = jnp.where(hit, below, levels[m])
    return out


def _top_rows(s, k, rows_ref):
    blocks = [s[v * 8:(v + 1) * 8] for v in range(s.shape[0] // 8)]
    for i, j in _sorting_network(len(blocks)):
        blocks[i], blocks[j] = jnp.maximum(blocks[i], blocks[j]), jnp.minimum(blocks[i], blocks[j])
    _extract_top(blocks, k, rows_ref)


def _count_true(pred, rows):
    c8 = pred(rows[7])
    c4 = pred(jnp.where(c8, rows[11], rows[3]))
    c2 = pred(jnp.where(c8, jnp.where(c4, rows[13], rows[9]), jnp.where(c4, rows[5], rows[1])))
    hi = jnp.where(c4, jnp.where(c2, rows[14], rows[12]), jnp.where(c2, rows[10], rows[8]))
    lo = jnp.where(c4, jnp.where(c2, rows[6], rows[4]), jnp.where(c2, rows[2], rows[0]))
    c1 = pred(jnp.where(c8, hi, lo))
    c16 = pred(rows[15])
    one = lambda c, w: jnp.where(c, float(w), 0.0)
    return (one(c8, 8) + one(c4, 4)) + (one(c2, 2) + one(c1, 1)) + one(c16, 1)


def _pair_words(x):
    u = pltpu.bitcast(x.astype(BF16).astype(F32), jnp.uint32)
    return u | lax.shift_right_logical(u, jnp.uint32(16))


def _route_head(hd, s1, s2, v1_ref, v2_ref, r2_ref, bb_ref, nn_ref, aa_ref, cols):
    _top_rows(s1, PEER_TOPK, v1_ref)
    _top_rows(s2, PEER_TOPK, v2_ref)
    v1 = v1_ref[...]
    v2 = v2_ref[...]
    v2_rows = [v2[k:k + 1] for k in range(PEER_TOPK)]
    k1 = lax.broadcasted_iota(jnp.int32, v1.shape, 0)
    levels = [jnp.where((k1 + 1) * (k2 + 1) <= PEER_TOPK, v1 + v2_rows[k2], -jnp.inf)
              for k2 in range(PEER_TOPK)]
    best = _extract_top(levels, PEER_TOPK)
    tau = best[-1]
    z = jnp.ones_like(tau)
    for r in range(1, PEER_TOPK):
        z = z + jnp.exp(best[r] - best[0])
    n = _count_true(lambda row: s1 + row >= tau, v2_rows)
    rank2 = _count_true(lambda row: row > s2, v2_rows)
    a = jnp.exp(s1 - v1[0:1]) * (1.0 / z)
    r2_ref[hd, :, cols] = rank2.astype(BF16)
    bb_ref[hd, :, cols] = jnp.exp(s2 - v2[0:1]).astype(BF16)
    n_words = _pair_words(n)
    a_words = _pair_words(a)
    for k in range(N_KEYS // PEER_KEY_ROWS):
        nn_ref[hd, k, :, cols] = n_words[k * PEER_KEY_ROWS:(k + 1) * PEER_KEY_ROWS]
        aa_ref[hd, k, :, cols] = a_words[k * PEER_KEY_ROWS:(k + 1) * PEER_KEY_ROWS]


def _peer_kernel(x_ref, mod_ref, g2_ref, wqt_ref, sk_ref, u_ref, vt_ref, fin_ref, out_ref,
                 hb_ref, r2_ref, bb_ref, nn_ref, aa_ref, v1_ref, v2_ref, pre_ref, wt_ref, acc_ref,
                 *, final_norm):
    c = pl.program_id(1)
    t = x_ref.shape[0]

    @pl.when(c == 0)
    def _routing():
        m = mod_ref[0]
        h = _rms(x_ref[...]) * g2_ref[...] * (1.0 + m[4:5]) + m[3:4]
        hb = h.astype(BF16)
        hb_ref[...] = hb
        qt = _dot_nt(wqt_ref[...], hb)
        for hd in range(PEER_HEADS):
            base = hd * 2 * PEER_HALF
            s1 = _dot(sk_ref[0], qt[base:base + PEER_HALF].astype(BF16))
            s2 = _dot(sk_ref[1], qt[base + PEER_HALF:base + 2 * PEER_HALF].astype(BF16))
            for lt in range(t // LANES):
                cols = slice(lt * LANES, (lt + 1) * LANES)
                _route_head(hd, s1[:, cols], s2[:, cols], v1_ref, v2_ref, r2_ref, bb_ref, nn_ref, aa_ref, cols)
        acc_ref[...] = jnp.zeros(acc_ref.shape, F32)

    pre_ref[...] = _dot_nt(u_ref[...], hb_ref[...])

    sub = BF16_ROWS
    for r in range(PEER_KEY_ROWS):
        for lt in range(t // LANES):
            cols = slice(lt * LANES, (lt + 1) * LANES)
            n_b, a_b = [], []
            for hd in range(PEER_HEADS):
                n_row = jnp.broadcast_to(nn_ref[hd, c, r:r + 1, cols], (sub // 2, LANES))
                a_row = jnp.broadcast_to(aa_ref[hd, c, r:r + 1, cols], (sub // 2, LANES))
                n_b.append(pltpu.bitcast(n_row, BF16))
                a_b.append(pltpu.bitcast(a_row, BF16))
            for jb in range(N_KEYS // sub):
                keys = slice(jb * sub, (jb + 1) * sub)
                g = jnp.zeros((sub, LANES), BF16)
                for hd in range(PEER_HEADS):
                    kept = jnp.where(r2_ref[hd, keys, cols] < n_b[hd], bb_ref[hd, keys, cols],
                                     jnp.zeros((sub, LANES), BF16))
                    g = g + kept * a_b[hd]
                rows = slice(r * N_KEYS + jb * sub, r * N_KEYS + (jb + 1) * sub)
                wt_ref[rows, cols] = _gelu_folded(pre_ref[rows, cols].astype(BF16)) * g
    acc_ref[...] += _dot(vt_ref[...], wt_ref[...])

    @pl.when(c == pl.num_programs(1) - 1)
    def _finish():
        y = x_ref[...] + mod_ref[0][5:6] * acc_ref[...].T
        if final_norm:
            y = _rms(y) * fin_ref[...]
        out_ref[...] = y


def _peer(x, mod_l, row_of_block, g2, wqt, sk, u, vt, fin, final_norm):
    n, d = x.shape
    t = PEER_ROWS
    ec = PEER_KEY_ROWS * N_KEYS
    nc = u.shape[0] // ec
    const2 = lambda i, c: (0, 0)
    gate = pltpu.VMEM((PEER_HEADS, N_KEYS, t), BF16)
    gate_rows = pltpu.VMEM((PEER_HEADS, N_KEYS // PEER_KEY_ROWS, PEER_KEY_ROWS, t), jnp.uint32)
    return pl.pallas_call(
        functools.partial(_peer_kernel, final_norm=final_norm),
        grid=(n // t, nc),
        in_specs=[
            pl.BlockSpec((t, d), lambda i, c: (i, 0)),
            pl.BlockSpec((1, ADA_CHUNKS, d), lambda i, c: (row_of_block(i), 0, 0)),
            pl.BlockSpec((1, d), const2),
            pl.BlockSpec(wqt.shape, const2),
            pl.BlockSpec(sk.shape, lambda i, c: (0, 0, 0)),
            pl.BlockSpec((ec, d), lambda i, c: (c, 0)),
            pl.BlockSpec((None, d, ec), lambda i, c: (c, 0, 0)),
            pl.BlockSpec((1, d), const2),
        ],
        out_specs=pl.BlockSpec((t, d), lambda i, c: (i, 0)),
        out_shape=jax.ShapeDtypeStruct((n, d), F32),
        scratch_shapes=[
            pltpu.VMEM((t, d), BF16),
            gate, gate, gate_rows, gate_rows,
            pltpu.VMEM((PEER_TOPK, LANES), F32),
            pltpu.VMEM((PEER_TOPK, LANES), F32),
            pltpu.VMEM((ec, t), F32),
            pltpu.VMEM((ec, t), BF16),
            pltpu.VMEM((d, t), F32),
        ],
        compiler_params=_params(2),
        name="peer",
    )(x, mod_l, g2, wqt, sk, u, vt, fin)


def _rope_tables(n_tokens, dim):
    rows = n_tokens // GRID_W
    row = jnp.repeat(jnp.arange(rows, dtype=F32), GRID_W)
    col = jnp.tile(jnp.arange(GRID_W, dtype=F32), rows)
    quarter = dim // 4
    inv = ROPE_THETA ** (-jnp.arange(quarter, dtype=F32) / quarter)
    ang = jnp.concatenate([row[:, None] * inv, col[:, None] * inv], axis=-1)
    cos, sin = jnp.cos(ang), jnp.sin(ang)
    reps = LANES // dim
    return (jnp.tile(jnp.concatenate([cos, cos], axis=-1), (1, reps)),
            jnp.tile(jnp.concatenate([-sin, sin], axis=-1), (1, reps)))


def _slot_pad(a, axis, n_heads):
    parts = []
    for hq in range(n_heads):
        blk = lax.slice_in_dim(a, hq * HEAD_DIM, (hq + 1) * HEAD_DIM, axis=axis)
        zero = jnp.zeros_like(blk)
        parts += [blk, zero] if (hq // GQA_GROUP) == 0 else [zero, blk]
    return jnp.concatenate(parts, axis=axis)


def kernel(x_prompt, x_sample, cache_diff_k, cache_diff_v, cache_gqa_k, cache_gqa_v, c, c_ctx, ada_w, ada_b, norm1_g, norm2_g, w_in, sgu_norm_g, sgu_w, sgu_b, diff_lq1, diff_lk1, diff_lq2, diff_lk2, diff_subln_g, gqa_qnorm_g, gqa_knorm_g, w_out, peer_wq, peer_subkeys, peer_u, peer_v, final_g):
    depth, d, _ = w_in.shape
    bp, lp, _ = x_prompt.shape
    bs, ls, _ = x_sample.shape
    past = cache_diff_k.shape[2]
    sgu_cols = sgu_norm_g.shape[1]
    diff_cols = diff_subln_g.shape[1] * cache_diff_v.shape[3]
    kv_cols = cache_gqa_k.shape[3] * HEAD_DIM
    gq_cols = kv_cols * GQA_GROUP
    n_gqa_heads = gq_cols // HEAD_DIM
    assert kv_cols == LANES and diff_cols == 2 * LANES and sgu_cols == 2 * LANES

    n_rows = -(-(1 + bs) // 8) * 8
    cvecs = jnp.zeros((n_rows, d), F32).at[0].set(c_ctx).at[1:1 + bs].set(c)
    mod = _modulation(cvecs, ada_w, ada_b).reshape(depth, n_rows, ADA_CHUNKS, d)

    o = 2 * sgu_cols + 3 * diff_cols
    win_ext = jnp.concatenate(
        [w_in[:, :, :o], _slot_pad(w_in[:, :, o:o + gq_cols], 2, n_gqa_heads), w_in[:, :, o + gq_cols:]],
        axis=2).astype(BF16)
    woab = w_out[:, :sgu_cols + diff_cols].astype(BF16)
    woc = _slot_pad(w_out[:, sgu_cols + diff_cols:], 1, n_gqa_heads).astype(BF16)
    sgw = sgu_w.astype(BF16)
    sgb = jnp.repeat(jnp.swapaxes(sgu_b, 1, 2), HEAD_DIM, axis=2)
    qng = jnp.tile(gqa_qnorm_g, (1, 2))[:, None, :]
    kng = jnp.tile(gqa_knorm_g, (1, 2))[:, None, :]
    subln = jnp.tile(diff_subln_g, (1, diff_cols // HEAD_DIM))[:, None, :]
    lam_vecs = jnp.stack([diff_lq1, diff_lk1, diff_lq2, diff_lk2], axis=1)
    wqt = jnp.swapaxes(peer_wq, 1, 2).astype(BF16)
    sk = peer_subkeys.astype(BF16)
    u_tab = peer_u.astype(BF16)
    ec = PEER_KEY_ROWS * N_KEYS
    vt_tab = jnp.swapaxes(peer_v.astype(BF16).reshape(depth, -1, ec, d), 2, 3)
    tabs = _rope_tables(ls, DIFF_DQK) + _rope_tables(ls, HEAD_DIM)

    def run(x3, sample):
        b, l, _ = x3.shape
        x = x3.reshape(b * l, d)
        if sample:
            row_front = lambda i: 1 + i // (l // FRONT_ROWS)
            row_attn = lambda i: 1 + i
            row_peer = lambda i: 1 + i // (l // PEER_ROWS)
        else:
            row_front = row_attn = row_peer = lambda i: 0
        kv_out = []
        for layer in range(depth):
            lam_init = 0.8 - 0.6 * math.exp(-0.3 * layer)
            mod_l = mod[layer]
            outs = _front(x, mod_l, row_front, norm1_g[layer][None], win_ext[layer], sgu_norm_g[layer][None],
                          sgw[layer], sgb[layer], qng[layer], kng[layer], tabs if sample else None, not sample)
            oa, dq, dk, dv, gq, gk, gv = [a.reshape(b, l, a.shape[1]) for a in outs[:7]]
            if sample:
                flat = lambda old: old[:, layer].reshape(b, past, -1).astype(BF16)
                cat = lambda new, old: jnp.concatenate([new, old], axis=1)
                ones = jnp.ones((b, past, LANES), BF16)
                cdv, cgv = flat(cache_diff_v), flat(cache_gqa_v)
                dk, gk = cat(dk, flat(cache_diff_k)), cat(gk, flat(cache_gqa_k))
                dv = cat(dv, jnp.concatenate([cdv[..., :LANES], ones, cdv[..., LANES:], ones], axis=-1))
                gv = cat(gv, jnp.concatenate([cgv, ones], axis=-1))
            else:
                kv_out.append(outs[7:])
            x3n = _attn(x.reshape(b, l, d), mod_l, row_attn, oa, dq, gq, dk, dv, gk, gv,
                        lam_vecs[layer], subln[layer], woab[layer], woc[layer], lam_init)
            x = _peer(x3n.reshape(b * l, d), mod_l, row_peer, norm2_g[layer][None], wqt[layer], sk[layer],
                      u_tab[layer], vt_tab[layer], final_g[None], layer == depth - 1)
        return x.reshape(b, l, d), kv_out

    y_prompt, kv = run(x_prompt, False)
    y_sample, _ = run(x_sample, True)
    stack = lambda idx, shape: jnp.stack([kv[layer][idx].reshape(shape) for layer in range(depth)], axis=1)
    new_diff_k = stack(0, (bp, lp) + cache_diff_k.shape[3:])
    new_diff_v = stack(1, (bp, lp) + cache_diff_v.shape[3:])
    new_gqa_k = stack(2, (bp, lp) + cache_gqa_k.shape[3:])
    new_gqa_v = stack(3, (bp, lp) + cache_gqa_v.shape[3:])
    return (y_prompt, y_sample, new_diff_k, new_diff_v, new_gqa_k, new_gqa_v)
```

```python
import functools
import math

import jax
import jax.numpy as jnp
from jax import lax
from jax.experimental import pallas as pl
from jax.experimental.pallas import tpu as pltpu

F32 = jnp.float32
BF16 = jnp.bfloat16

HEAD_DIM = 64
SGU_CHUNK = 128
DIFF_DQK = 32
GQA_GROUP = 4
ADA_CHUNKS = 6
N_KEYS = 128
PEER_HEADS = 8
PEER_TOPK = 16
PEER_HALF = 64
GRID_W = 64
ROPE_THETA = 10000.0
EPS = 1e-6
DIFF_SCALE = DIFF_DQK ** -0.5
GQA_SCALE = HEAD_DIM ** -0.5

LANES = 128
BF16_ROWS = 16
VMEM_LIMIT_BYTES = 56 * 1024 * 1024

FRONT_ROWS = 256
ATTN_ROWS = 256
PEER_ROWS = 512
PEER_KEY_ROWS = 16


def _dot(a, b):
    return jnp.dot(a, b, preferred_element_type=F32)


def _dot_nt(a, b):
    return lax.dot_general(a, b, (((1,), (1,)), ((), ())), preferred_element_type=F32)


def _rms(x):
    return x * lax.rsqrt(jnp.mean(x * x, axis=-1, keepdims=True) + EPS)


def _gelu(x):
    c = math.sqrt(2.0 / math.pi)
    return 0.5 * x * (1.0 + jnp.tanh(c * (x + 0.044715 * (x * x * x))))


def _gelu_folded(x):
    c = math.sqrt(2.0 / math.pi)
    hx = 0.5 * x.astype(BF16)
    return hx + hx * jnp.tanh((x * (c + (0.044715 * c) * (x * x))).astype(BF16))


def _params(n_axes):
    return pltpu.CompilerParams(
        dimension_semantics=("arbitrary",) * n_axes, vmem_limit_bytes=VMEM_LIMIT_BYTES)


def _mod_kernel(c_ref, w_ref, b_ref, o_ref):
    c = c_ref[...]
    s = c / (1.0 + jnp.exp(-c))
    o_ref[0] = _dot(s.astype(BF16), w_ref[0].astype(BF16)) + b_ref[0]


def _modulation(cvecs, ada_w, ada_b):
    depth, d, n = ada_w.shape
    rows = cvecs.shape[0]
    tn = n // 4
    return pl.pallas_call(
        _mod_kernel,
        grid=(depth, n // tn),
        in_specs=[
            pl.BlockSpec((rows, d), lambda l, j: (0, 0)),
            pl.BlockSpec((1, d, tn), lambda l, j: (l, 0, j)),
            pl.BlockSpec((1, 1, tn), lambda l, j: (l, 0, j)),
        ],
        out_specs=pl.BlockSpec((1, rows, tn), lambda l, j: (l, 0, j)),
        out_shape=jax.ShapeDtypeStruct((depth, rows, n), F32),
        compiler_params=_params(2),
        name="modulation",
    )(cvecs, ada_w, ada_b.reshape(depth, 1, n))


def _rope(x, cos, sin_signed, half):
    lane = lax.broadcasted_iota(jnp.int32, (1, LANES), 1)
    first = (lane % (2 * half)) < half
    partner = jnp.where(first, pltpu.roll(x, LANES - half, 1), pltpu.roll(x, half, 1))
    return x * cos + partner * sin_signed


def _front_kernel(*refs, rope, emit_kv, sgu_w_cols):
    (x_ref, mod_ref, g1_ref, win_ref, sgn_ref, sgw_ref, sgb_ref, qng_ref, kng_ref) = refs[:9]
    pos = 9
    if rope:
        cd_ref, sd_ref, cg_ref, sg_ref = refs[pos:pos + 4]
        pos += 4
    oa_ref, dq_ref, dk_ref, dv_ref, gq_ref, gk_ref, gv_ref = refs[pos:pos + 7]
    pos += 7
    if emit_kv:
        kd32_ref, vd32_ref, kg32_ref, vg32_ref = refs[pos:pos + 4]

    x = x_ref[...]
    m = mod_ref[0]
    h = _rms(x) * g1_ref[...] * (1.0 + m[1:2]) + m[0:1]
    proj = _dot(h.astype(BF16), win_ref[...])

    w = sgu_w_cols
    a_u = _gelu(proj[:, 0:w])
    a_v = _rms(_gelu(proj[:, w:2 * w])) * sgn_ref[...]
    a_vb = a_v.astype(BF16)
    lane_w = lax.broadcasted_iota(jnp.int32, (1, w), 1)
    n_groups = w // HEAD_DIM
    t = x.shape[0]
    for ch in range(t // SGU_CHUNK):
        rows = slice(ch * SGU_CHUNK, (ch + 1) * SGU_CHUNK)
        av = a_vb[rows]
        mix = jnp.zeros((SGU_CHUNK, w), F32)
        for g in range(n_groups):
            part = _dot(sgw_ref[g], av)
            mix = jnp.where(lane_w // HEAD_DIM == g, part, mix)
        oa_ref[rows, :] = (a_u[rows] * (mix + sgb_ref[...])).astype(BF16)

    dw = 2 * LANES
    o = 2 * w
    d_q = proj[:, o:o + dw]
    d_k = proj[:, o + dw:o + 2 * dw]
    d_v = proj[:, o + 2 * dw:o + 3 * dw]
    o += 3 * dw
    gw = proj.shape[1] - o - 2 * LANES
    g_q = proj[:, o:o + gw]
    g_k = proj[:, o + gw:o + gw + LANES]
    g_v = proj[:, o + gw + LANES:o + gw + 2 * LANES]

    lane = lax.broadcasted_iota(jnp.int32, (1, LANES), 1)
    low = lane < HEAD_DIM
    k2 = g_k * g_k
    s_low = jnp.sum(jnp.where(low, k2, 0.0), axis=-1, keepdims=True)
    s_all = jnp.sum(k2, axis=-1, keepdims=True)
    ms = jnp.where(low, s_low, s_all - s_low) * (1.0 / HEAD_DIM)
    g_k = g_k * lax.rsqrt(ms + EPS) * kng_ref[...]

    if emit_kv:
        kd32_ref[...] = d_k
        vd32_ref[...] = d_v
        kg32_ref[...] = g_k
        vg32_ref[...] = g_v

    if rope:
        g_k = _rope(g_k, cg_ref[...], sg_ref[...], HEAD_DIM // 2)
    gk_ref[...] = g_k.astype(BF16)
    ones = jnp.ones((x.shape[0], LANES), BF16)
    gv_ref[...] = jnp.concatenate([g_v.astype(BF16), ones], axis=-1)
    dv_ref[...] = jnp.concatenate(
        [d_v[:, :LANES].astype(BF16), ones, d_v[:, LANES:].astype(BF16), ones], axis=-1)

    for s in range(2):
        cols = slice(s * LANES, (s + 1) * LANES)
        q = d_q[:, cols]
        k = d_k[:, cols]
        if rope:
            q = _rope(q, cd_ref[...], sd_ref[...], DIFF_DQK // 2)
            k = _rope(k, cd_ref[...], sd_ref[...], DIFF_DQK // 2)
        dq_ref[:, cols] = (q * DIFF_SCALE).astype(BF16)
        dk_ref[:, cols] = k.astype(BF16)

    for s in range(g_q.shape[1] // LANES):
        cols = slice(s * LANES, (s + 1) * LANES)
        q = g_q[:, cols]
        ms = jnp.sum(q * q, axis=-1, keepdims=True) * (1.0 / HEAD_DIM)
        q = q * lax.rsqrt(ms + EPS) * qng_ref[...]
        if rope:
            q = _rope(q, cg_ref[...], sg_ref[...], HEAD_DIM // 2)
        gq_ref[:, cols] = (q * GQA_SCALE).astype(BF16)


def _front(x, mod_l, row_of_block, g1, win, sgn, sgw, sgb, qng, kng, rope_tabs, emit_kv):
    n, d = x.shape
    t = FRONT_ROWS
    nb = n // t
    rope = rope_tabs is not None
    const2 = lambda i: (0, 0)
    in_specs = [
        pl.BlockSpec((t, d), lambda i: (i, 0)),
        pl.BlockSpec((1, ADA_CHUNKS, d), lambda i: (row_of_block(i), 0, 0)),
        pl.BlockSpec((1, d), const2),
        pl.BlockSpec(win.shape, const2),
        pl.BlockSpec(sgn.shape, const2),
        pl.BlockSpec(sgw.shape, lambda i: (0, 0, 0)),
        pl.BlockSpec(sgb.shape, const2),
        pl.BlockSpec(qng.shape, const2),
        pl.BlockSpec(kng.shape, const2),
    ]
    args = [x, mod_l, g1, win, sgn, sgw, sgb, qng, kng]
    if rope:
        blocks_per_seq = rope_tabs[0].shape[0] // t
        for tab in rope_tabs:
            in_specs.append(pl.BlockSpec((t, LANES), lambda i: (i % blocks_per_seq, 0)))
            args.append(tab)
    widths = [(256, BF16), (256, BF16), (256, BF16), (512, BF16), (1024, BF16), (128, BF16), (256, BF16)]
    if emit_kv:
        widths += [(256, F32), (256, F32), (128, F32), (128, F32)]
    out_specs = [pl.BlockSpec((t, wd), lambda i: (i, 0)) for wd, _ in widths]
    out_shape = [jax.ShapeDtypeStruct((n, wd), dt) for wd, dt in widths]
    return pl.pallas_call(
        functools.partial(_front_kernel, rope=rope, emit_kv=emit_kv, sgu_w_cols=sgn.shape[1]),
        grid=(nb,),
        in_specs=in_specs,
        out_specs=out_specs,
        out_shape=out_shape,
        compiler_params=_params(1),
        name="front",
    )(*args)


def _softmax_pv(s, v_ones):
    sb = s.astype(BF16)
    e = jnp.exp(sb - jnp.max(sb, axis=-1, keepdims=True))
    pv = _dot(e, v_ones)
    return pv[:, :LANES] * (1.0 / pv[:, LANES:])


def _attn_kernel(x_ref, mod_ref, oa_ref, dq_ref, gq_ref, dk_ref, dv_ref, gk_ref, gv_ref,
                 lam_ref, subln_ref, woab_ref, woc_ref, out_ref, oc_ref, *, lam_init):
    lam = (jnp.exp(jnp.sum(lam_ref[0:1] * lam_ref[1:2], axis=-1, keepdims=True))
           - jnp.exp(jnp.sum(lam_ref[2:3] * lam_ref[3:4], axis=-1, keepdims=True)) + lam_init)

    dq = dq_ref[...]
    dk = dk_ref[...]
    w = dq.shape[1]
    lane = lax.broadcasted_iota(jnp.int32, (1, w), 1)
    lane_pair = lax.broadcasted_iota(jnp.int32, (1, LANES), 1)
    heads_per_pair = LANES // HEAD_DIM
    pairs = []
    for pr in range(w // LANES):
        dv = dv_ref[:, pr * 2 * LANES:(pr + 1) * 2 * LANES]
        slot = jnp.zeros((dq.shape[0], LANES), F32)
        inv = jnp.zeros((dq.shape[0], LANES), F32)
        for sub in range(heads_per_pair):
            hd = pr * heads_per_pair + sub
            parts = []
            for xx in range(2):
                sel = (lane // DIFF_DQK) == (2 * hd + xx)
                qm = jnp.where(sel, dq, jnp.zeros_like(dq))
                parts.append(_softmax_pv(_dot_nt(qm, dk), dv))
            o = parts[0] - lam * parts[1]
            in_head = (lane_pair // HEAD_DIM) == sub
            ms = jnp.sum(jnp.where(in_head, o * o, 0.0), axis=-1, keepdims=True) * (1.0 / HEAD_DIM)
            slot = jnp.where(in_head, o, slot)
            inv = jnp.where(in_head, lax.rsqrt(ms + EPS), inv)
        pairs.append(slot * inv)
    out_b = jnp.concatenate(pairs, axis=-1) * subln_ref[...] * (1.0 - lam_init)

    gk = gk_ref[...]
    gv = gv_ref[...]
    for hq in range(gq_ref.shape[1] // LANES):
        cols = slice(hq * LANES, (hq + 1) * LANES)
        oc_ref[:, cols] = _softmax_pv(_dot_nt(gq_ref[:, cols], gk), gv).astype(BF16)

    ab = jnp.concatenate([oa_ref[...], out_b.astype(BF16)], axis=-1)
    mix = _dot(ab, woab_ref[...]) + _dot(oc_ref[...], woc_ref[...])
    out_ref[...] = x_ref[...] + mod_ref[0][2:3] * mix


def _attn(x3, mod_l, row_of_batch, oa, dq, gq, dk, dv, gk, gv, lam_vecs, subln, woab, woc, lam_init):
    b, l, d = x3.shape
    tq = min(ATTN_ROWS, l)
    lk = dk.shape[1]
    qspec = lambda wd: pl.BlockSpec((None, tq, wd), lambda i, j: (i, j, 0))
    kspec = lambda wd: pl.BlockSpec((None, lk, wd), lambda i, j: (i, 0, 0))
    const2 = lambda i, j: (0, 0)
    return pl.pallas_call(
        functools.partial(_attn_kernel, lam_init=lam_init),
        grid=(b, l // tq),
        in_specs=[
            qspec(d),
            pl.BlockSpec((1, ADA_CHUNKS, d), lambda i, j: (row_of_batch(i), 0, 0)),
            qspec(oa.shape[2]), qspec(dq.shape[2]), qspec(gq.shape[2]),
            kspec(dk.shape[2]), kspec(dv.shape[2]), kspec(gk.shape[2]), kspec(gv.shape[2]),
            pl.BlockSpec(lam_vecs.shape, const2),
            pl.BlockSpec(subln.shape, const2),
            pl.BlockSpec(woab.shape, const2),
            pl.BlockSpec(woc.shape, const2),
        ],
        out_specs=qspec(d),
        out_shape=jax.ShapeDtypeStruct((b, l, d), F32),
        scratch_shapes=[pltpu.VMEM((tq, gq.shape[2]), BF16)],
        compiler_params=_params(2),
        name="attn",
    )(x3, mod_l, oa, dq, gq, dk, dv, gk, gv, lam_vecs, subln, woab, woc)


def _sorting_network(n):
    pairs = []

    def merge(lo, hi, r):
        step = r * 2
        if step < hi - lo:
            merge(lo, hi, step)
            merge(lo + r, hi, step)
            pairs.extend((i, i + r) for i in range(lo + r, hi - r, step))
        else:
            pairs.append((lo, lo + r))

    def sort(lo, hi):
        if hi - lo >= 1:
            mid = lo + (hi - lo) // 2
            sort(lo, mid)
            sort(mid + 1, hi)
            merge(lo, hi, 1)

    sort(0, n - 1)
    return pairs


def _extract_top(levels, k, rows_ref=None):
    levels = list(levels)
    out = []
    for r in range(k):
        head = levels[0]
        mx = jnp.max(head, axis=0, keepdims=True)
        out.append(mx)
        if rows_ref is not None:
            rows_ref[r:r + 1, :] = mx
        need = k - r - 1
        if need:
            hit = head == mx
            for m in range(min(need, len(levels))):
                below = levels[m + 1] if m + 1 < len(levels) else -jnp.inf
                levels[m] = jnp.where(hit, below, levels[m])
    return out


def _top_rows(s, k, rows_ref):
    blocks = [s[v * 8:(v + 1) * 8] for v in range(s.shape[0] // 8)]
    for i, j in _sorting_network(len(blocks)):
        blocks[i], blocks[j] = jnp.maximum(blocks[i], blocks[j]), jnp.minimum(blocks[i], blocks[j])
    _extract_top(blocks, k, rows_ref)


def _count_true(pred, rows):
    c8 = pred(rows[7])
    c4 = pred(jnp.where(c8, rows[11], rows[3]))
    c2 = pred(jnp.where(c8, jnp.where(c4, rows[13], rows[9]), jnp.where(c4, rows[5], rows[1])))
    hi = jnp.where(c4, jnp.where(c2, rows[14], rows[12]), jnp.where(c2, rows[10], rows[8]))
    lo = jnp.where(c4, jnp.where(c2, rows[6], rows[4]), jnp.where(c2, rows[2], rows[0]))
    c1 = pred(jnp.where(c8, hi, lo))
    c16 = pred(rows[15])
    one = lambda c, w: jnp.where(c, float(w), 0.0)
    return (one(c8, 8) + one(c4, 4)) + (one(c2, 2) + one(c1, 1)) + one(c16, 1)


def _route_head(hd, s1, s2, v1_ref, v2_ref, r2_ref, bb_ref, nn_ref, aa_ref, cols):
    _top_rows(s1, PEER_TOPK, v1_ref)
    _top_rows(s2, PEER_TOPK, v2_ref)
    v1 = v1_ref[...]
    v2 = v2_ref[...]
    v2_rows = [v2[k:k + 1] for k in range(PEER_TOPK)]
    k1 = lax.broadcasted_iota(jnp.int32, v1.shape, 0)
    levels = [jnp.where((k1 + 1) * (k2 + 1) <= PEER_TOPK, v1 + v2_rows[k2], -jnp.inf)
              for k2 in range(PEER_TOPK)]
    best = _extract_top(levels, PEER_TOPK)
    tau = best[-1]
    z = jnp.ones_like(tau)
    for r in range(1, PEER_TOPK):
        z = z + jnp.exp(best[r] - best[0])
    n = _count_true(lambda row: s1 + row >= tau, v2_rows)
    rank2 = _count_true(lambda row: row > s2, v2_rows)
    a = jnp.exp(s1 - v1[0:1]) * (1.0 / z)
    r2_ref[hd, :, cols] = rank2.astype(BF16)
    bb_ref[hd, :, cols] = jnp.exp(s2 - v2[0:1]).astype(BF16)
    for k in range(N_KEYS // PEER_KEY_ROWS):
        nn_ref[hd, k, :, cols] = n[k * PEER_KEY_ROWS:(k + 1) * PEER_KEY_ROWS]
        aa_ref[hd, k, :, cols] = a[k * PEER_KEY_ROWS:(k + 1) * PEER_KEY_ROWS]


def _peer_kernel(x_ref, mod_ref, g2_ref, wqt_ref, sk_ref, u_ref, vt_ref, fin_ref, out_ref,
                 hb_ref, r2_ref, bb_ref, nn_ref, aa_ref, v1_ref, v2_ref, pre_ref, wt_ref, acc_ref,
                 *, final_norm):
    c = pl.program_id(1)
    t = x_ref.shape[0]

    @pl.when(c == 0)
    def _routing():
        m = mod_ref[0]
        h = _rms(x_ref[...]) * g2_ref[...] * (1.0 + m[4:5]) + m[3:4]
        hb = h.astype(BF16)
        hb_ref[...] = hb
        qt = _dot_nt(wqt_ref[...], hb)
        for hd in range(PEER_HEADS):
            base = hd * 2 * PEER_HALF
            s1 = _dot(sk_ref[0], qt[base:base + PEER_HALF].astype(BF16))
            s2 = _dot(sk_ref[1], qt[base + PEER_HALF:base + 2 * PEER_HALF].astype(BF16))
            for lt in range(t // LANES):
                cols = slice(lt * LANES, (lt + 1) * LANES)
                _route_head(hd, s1[:, cols], s2[:, cols], v1_ref, v2_ref, r2_ref, bb_ref, nn_ref, aa_ref, cols)
        acc_ref[...] = jnp.zeros(acc_ref.shape, F32)

    pre_ref[...] = _dot_nt(u_ref[...], hb_ref[...])

    sub = BF16_ROWS
    for r in range(PEER_KEY_ROWS):
        for lt in range(t // LANES):
            cols = slice(lt * LANES, (lt + 1) * LANES)
            n_b, a_b = [], []
            for hd in range(PEER_HEADS):
                n_row = jnp.broadcast_to(nn_ref[hd, c, r:r + 1, cols], (sub, LANES))
                a_row = jnp.broadcast_to(aa_ref[hd, c, r:r + 1, cols], (sub, LANES))
                n_b.append(n_row.astype(BF16))
                a_b.append(a_row.astype(BF16))
            for jb in range(N_KEYS // sub):
                keys = slice(jb * sub, (jb + 1) * sub)
                g = jnp.zeros((sub, LANES), BF16)
                for hd in range(PEER_HEADS):
                    kept = jnp.where(r2_ref[hd, keys, cols] < n_b[hd], bb_ref[hd, keys, cols],
                                     jnp.zeros((sub, LANES), BF16))
                    g = g + kept * a_b[hd]
                rows = slice(r * N_KEYS + jb * sub, r * N_KEYS + (jb + 1) * sub)
                wt_ref[rows, cols] = _gelu_folded(pre_ref[rows, cols]) * g
    acc_ref[...] += _dot(vt_ref[...], wt_ref[...])

    @pl.when(c == pl.num_programs(1) - 1)
    def _finish():
        y = x_ref[...] + mod_ref[0][5:6] * acc_ref[...].T
        if final_norm:
            y = _rms(y) * fin_ref[...]
        out_ref[...] = y


def _peer(x, mod_l, row_of_block, g2, wqt, sk, u, vt, fin, final_norm):
    n, d = x.shape
    t = PEER_ROWS
    ec = PEER_KEY_ROWS * N_KEYS
    nc = u.shape[0] // ec
    const2 = lambda i, c: (0, 0)
    gate = pltpu.VMEM((PEER_HEADS, N_KEYS, t), BF16)
    gate_rows = pltpu.VMEM((PEER_HEADS, N_KEYS // PEER_KEY_ROWS, PEER_KEY_ROWS, t), F32)
    return pl.pallas_call(
        functools.partial(_peer_kernel, final_norm=final_norm),
        grid=(n // t, nc),
        in_specs=[
            pl.BlockSpec((t, d), lambda i, c: (i, 0)),
            pl.BlockSpec((1, ADA_CHUNKS, d), lambda i, c: (row_of_block(i), 0, 0)),
            pl.BlockSpec((1, d), const2),
            pl.BlockSpec(wqt.shape, const2),
            pl.BlockSpec(sk.shape, lambda i, c: (0, 0, 0)),
            pl.BlockSpec((ec, d), lambda i, c: (c, 0)),
            pl.BlockSpec((None, d, ec), lambda i, c: (c, 0, 0)),
            pl.BlockSpec((1, d), const2),
        ],
        out_specs=pl.BlockSpec((t, d), lambda i, c: (i, 0)),
        out_shape=jax.ShapeDtypeStruct((n, d), F32),
        scratch_shapes=[
            pltpu.VMEM((t, d), BF16),
            gate, gate, gate_rows, gate_rows,
            pltpu.VMEM((PEER_TOPK, LANES), F32),
            pltpu.VMEM((PEER_TOPK, LANES), F32),
            pltpu.VMEM((ec, t), F32),
            pltpu.VMEM((ec, t), BF16),
            pltpu.VMEM((d, t), F32),
        ],
        compiler_params=_params(2),
        name="peer",
    )(x, mod_l, g2, wqt, sk, u, vt, fin)


def _rope_tables(n_tokens, dim):
    rows = n_tokens // GRID_W
    row = jnp.repeat(jnp.arange(rows, dtype=F32), GRID_W)
    col = jnp.tile(jnp.arange(GRID_W, dtype=F32), rows)
    quarter = dim // 4
    inv = ROPE_THETA ** (-jnp.arange(quarter, dtype=F32) / quarter)
    ang = jnp.concatenate([row[:, None] * inv, col[:, None] * inv], axis=-1)
    cos, sin = jnp.cos(ang), jnp.sin(ang)
    reps = LANES // dim
    return (jnp.tile(jnp.concatenate([cos, cos], axis=-1), (1, reps)),
            jnp.tile(jnp.concatenate([-sin, sin], axis=-1), (1, reps)))


def _slot_pad(a, axis, n_heads):
    parts = []
    for hq in range(n_heads):
        blk = lax.slice_in_dim(a, hq * HEAD_DIM, (hq + 1) * HEAD_DIM, axis=axis)
        zero = jnp.zeros_like(blk)
        parts += [blk, zero] if (hq // GQA_GROUP) == 0 else [zero, blk]
    return jnp.concatenate(parts, axis=axis)


def kernel(x_prompt, x_sample, cache_diff_k, cache_diff_v, cache_gqa_k, cache_gqa_v, c, c_ctx, ada_w, ada_b, norm1_g, norm2_g, w_in, sgu_norm_g, sgu_w, sgu_b, diff_lq1, diff_lk1, diff_lq2, diff_lk2, diff_subln_g, gqa_qnorm_g, gqa_knorm_g, w_out, peer_wq, peer_subkeys, peer_u, peer_v, final_g):
    depth, d, _ = w_in.shape
    bp, lp, _ = x_prompt.shape
    bs, ls, _ = x_sample.shape
    past = cache_diff_k.shape[2]
    sgu_cols = sgu_norm_g.shape[1]
    diff_cols = diff_subln_g.shape[1] * cache_diff_v.shape[3]
    kv_cols = cache_gqa_k.shape[3] * HEAD_DIM
    gq_cols = kv_cols * GQA_GROUP
    n_gqa_heads = gq_cols // HEAD_DIM
    assert kv_cols == LANES and diff_cols == 2 * LANES and sgu_cols == 2 * LANES

    n_rows = -(-(1 + bs) // 8) * 8
    cvecs = jnp.zeros((n_rows, d), F32).at[0].set(c_ctx).at[1:1 + bs].set(c)
    mod = _modulation(cvecs, ada_w, ada_b).reshape(depth, n_rows, ADA_CHUNKS, d)

    o = 2 * sgu_cols + 3 * diff_cols
    win_ext = jnp.concatenate(
        [w_in[:, :, :o], _slot_pad(w_in[:, :, o:o + gq_cols], 2, n_gqa_heads), w_in[:, :, o + gq_cols:]],
        axis=2).astype(BF16)
    woab = w_out[:, :sgu_cols + diff_cols].astype(BF16)
    woc = _slot_pad(w_out[:, sgu_cols + diff_cols:], 1, n_gqa_heads).astype(BF16)
    sgw = sgu_w.astype(BF16)
    sgb = jnp.repeat(jnp.swapaxes(sgu_b, 1, 2), HEAD_DIM, axis=2)
    qng = jnp.tile(gqa_qnorm_g, (1, 2))[:, None, :]
    kng = jnp.tile(gqa_knorm_g, (1, 2))[:, None, :]
    subln = jnp.tile(diff_subln_g, (1, diff_cols // HEAD_DIM))[:, None, :]
    lam_vecs = jnp.stack([diff_lq1, diff_lk1, diff_lq2, diff_lk2], axis=1)
    wqt = jnp.swapaxes(peer_wq, 1, 2).astype(BF16)
    sk = peer_subkeys.astype(BF16)
    u_tab = peer_u.astype(BF16)
    ec = PEER_KEY_ROWS * N_KEYS
    vt_tab = jnp.swapaxes(peer_v.astype(BF16).reshape(depth, -1, ec, d), 2, 3)
    tabs = _rope_tables(ls, DIFF_DQK) + _rope_tables(ls, HEAD_DIM)

    def run(x3, sample):
        b, l, _ = x3.shape
        x = x3.reshape(b * l, d)
        if sample:
            row_front = lambda i: 1 + i // (l // FRONT_ROWS)
            row_attn = lambda i: 1 + i
            row_peer = lambda i: 1 + i // (l // PEER_ROWS)
        else:
            row_front = row_attn = row_peer = lambda i: 0
        kv_out = []
        for layer in range(depth):
            lam_init = 0.8 - 0.6 * math.exp(-0.3 * layer)
            mod_l = mod[layer]
            outs = _front(x, mod_l, row_front, norm1_g[layer][None], win_ext[layer], sgu_norm_g[layer][None],
                          sgw[layer], sgb[layer], qng[layer], kng[layer], tabs if sample else None, not sample)
            oa, dq, dk, dv, gq, gk, gv = [a.reshape(b, l, a.shape[1]) for a in outs[:7]]
            if sample:
                flat = lambda old: old[:, layer].reshape(b, past, -1).astype(BF16)
                cat = lambda new, old: jnp.concatenate([new, old], axis=1)
                ones = jnp.ones((b, past, LANES), BF16)
                cdv, cgv = flat(cache_diff_v), flat(cache_gqa_v)
                dk, gk = cat(dk, flat(cache_diff_k)), cat(gk, flat(cache_gqa_k))
                dv = cat(dv, jnp.concatenate([cdv[..., :LANES], ones, cdv[..., LANES:], ones], axis=-1))
                gv = cat(gv, jnp.concatenate([cgv, ones], axis=-1))
            else:
                kv_out.append(outs[7:])
            x3n = _attn(x.reshape(b, l, d), mod_l, row_attn, oa, dq, gq, dk, dv, gk, gv,
                        lam_vecs[layer], subln[layer], woab[layer], woc[layer], lam_init)
            x = _peer(x3n.reshape(b * l, d), mod_l, row_peer, norm2_g[layer][None], wqt[layer], sk[layer],
                      u_tab[layer], vt_tab[layer], final_g[None], layer == depth - 1)
        return x.reshape(b, l, d), kv_out

    y_prompt, kv = run(x_prompt, False)
    y_sample, _ = run(x_sample, True)
    stack = lambda idx, shape: jnp.stack([kv[layer][idx].reshape(shape) for layer in range(depth)], axis=1)
    new_diff_k = stack(0, (bp, lp) + cache_diff_k.shape[3:])
    new_diff_v = stack(1, (bp, lp) + cache_diff_v.shape[3:])
    new_gqa_k = stack(2, (bp, lp) + cache_gqa_k.shape[3:])
    new_gqa_v = stack(3, (bp, lp) + cache_gqa_v.shape[3:])
    return (y_prompt, y_sample, new_diff_k, new_diff_v, new_gqa_k, new_gqa_v)
```
